```python
import jax, jax.numpy as jnp
from jax import lax
import numpy as np

D_MODEL = 1024
BATCH = 8
SEQ = 2048
DEPTH = 1
DEC_BATCH = 128
DEC_SEQ = 4
PAST_LEN = 16384
PAGE_SIZE = 128

GDN_HEADS = 8
GDN_DK = 128
GDN_DV = 128
GDN_CONV = 4
GDN_CHUNK = 64
QK_W = GDN_HEADS * GDN_DK
V_W = GDN_HEADS * GDN_DV
QKV_W = 2 * QK_W + V_W
CONF_DIM = D_MODEL
CONF_WIDTH = 31
MEM_LEN = 256
XA_HEADS = 4
XA_DH = D_MODEL // XA_HEADS
N_EXPERTS = 32
TOP_K = 4
D_FF = D_MODEL
SWIGLU_ALPHA = 1.702
SWIGLU_LIMIT = 7.0
SPLITS = (QKV_W, QKV_W + V_W, QKV_W + V_W + GDN_HEADS, QKV_W + V_W + 2 * GDN_HEADS,
          QKV_W + V_W + 2 * GDN_HEADS + 2 * CONF_DIM)
N_IN = SPLITS[-1] + 2 * D_MODEL

kernel_name = "hybrid_gdn_conformer_memxattn_moe_step"


def rms_norm(x, g, eps=1e-6):
    xf = x.astype(jnp.float32)
    y = xf * lax.rsqrt(jnp.mean(xf * xf, axis=-1, keepdims=True) + eps)
    return (y * g.astype(jnp.float32)).astype(x.dtype)


def layer_norm(x, g, b, eps=1e-5):
    xf = x.astype(jnp.float32)
    mu = jnp.mean(xf, axis=-1, keepdims=True)
    xc = xf - mu
    var = jnp.mean(xc * xc, axis=-1, keepdims=True)
    return (xc * lax.rsqrt(var + eps) * g.astype(jnp.float32) + b.astype(jnp.float32)).astype(x.dtype)


def l2_normalize(x, eps=1e-6):
    return x * lax.rsqrt(jnp.sum(x * x, axis=-1, keepdims=True) + eps)


def causal_dwconv(x, buf, w):
    ext = jnp.concatenate([buf.astype(x.dtype), x], axis=1)
    y = lax.conv_general_dilated(ext, w[:, None, :].astype(x.dtype), window_strides=(1,), padding="VALID",
                                 dimension_numbers=("NWC", "WIO", "NWC"), feature_group_count=x.shape[-1])
    return y, ext[:, -(w.shape[0] - 1):]


def gated_delta_chunked(q, k, v, g, beta, s0):
    bsz, t_len, n_h, _ = q.shape
    d_v = v.shape[-1]
    c = min(GDN_CHUNK, t_len)
    n_c = -(-t_len // c)
    pad = n_c * c - t_len
    if pad:
        padw = lambda a: jnp.pad(a, [(0, 0), (0, pad)] + [(0, 0)] * (a.ndim - 2))
        q, k, v, g, beta = padw(q), padw(k), padw(v), padw(g), padw(beta)

    def blocks(a):
        return jnp.moveaxis(a.reshape((bsz, n_c, c) + a.shape[2:]), 3, 1)

    q, k, v, g, beta = blocks(q), blocks(k), blocks(v), blocks(g), blocks(beta)
    gc = jnp.cumsum(g, axis=-1)
    incl = jnp.tril(jnp.ones((c, c), dtype=bool))
    strict = jnp.tril(jnp.ones((c, c), dtype=bool), -1)
    diff = gc[..., :, None] - gc[..., None, :]
    decay = jnp.where(incl, jnp.exp(jnp.where(incl, diff, 0.0)), 0.0)
    kb = k * beta[..., None]
    m = jnp.where(strict, jnp.einsum("bhncd,bhnsd->bhncs", kb, k) * decay, 0.0)
    a_mat = m + jnp.eye(c, dtype=m.dtype)
    rhs = jnp.concatenate([v * beta[..., None], kb * jnp.exp(gc)[..., None]], axis=-1)
    sol = lax.linalg.triangular_solve(a_mat, rhs, left_side=True, lower=True, unit_diagonal=True)
    u, w = sol[..., :d_v], sol[..., d_v:]
    p_intra = jnp.einsum("bhncd,bhnsd->bhncs", q, k) * decay

    def step(s, xs):
        qc, kc, uc, wc, gcc, pc = xs
        v_new = uc - jnp.einsum("bhcd,bhde->bhce", wc, s)
        o = (jnp.einsum("bhcd,bhde->bhce", qc * jnp.exp(gcc)[..., None], s)
             + jnp.einsum("bhcs,bhse->bhce", pc, v_new))
        g_last = gcc[..., -1]
        s = (s * jnp.exp(g_last)[..., None, None]
             + jnp.einsum("bhcd,bhce->bhde", kc * jnp.exp(g_last[..., None] - gcc)[..., None], v_new))
        return s, o

    xs = tuple(jnp.moveaxis(a, 2, 0) for a in (q, k, u, w, gc, p_intra))
    s_fin, o = lax.scan(step, s0, xs)
    o = jnp.moveaxis(jnp.moveaxis(o, 0, 2), 1, 3).reshape(bsz, n_c * c, n_h, d_v)[:, :t_len]
    return o, s_fin


def mixer(h, gdn_buf, s0, conf_buf, w_in, gdn_conv_w, gdn_a_log, gdn_dt_bias, gdn_norm, gdn_o,
          conf_conv_w, conf_conv_b, conf_ln_g, conf_ln_b, conf_pw2, conf_pw2_b, w_out):
    bsz, t_len, _ = h.shape
    f32 = jnp.float32
    proj = h @ w_in
    qkv, z, a, b, glu, gates = jnp.split(proj, SPLITS, axis=-1)
    qkv, gdn_buf_new = causal_dwconv(qkv, gdn_buf, gdn_conv_w)
    qkv = jax.nn.silu(qkv).astype(f32)
    q, k, v = jnp.split(qkv, (QK_W, 2 * QK_W), axis=-1)
    q = l2_normalize(q.reshape(bsz, t_len, GDN_HEADS, GDN_DK)) * (GDN_DK ** -0.5)
    k = l2_normalize(k.reshape(bsz, t_len, GDN_HEADS, GDN_DK))
    v = v.reshape(bsz, t_len, GDN_HEADS, GDN_DV)
    beta = jax.nn.sigmoid(b.astype(f32))
    g = -jnp.exp(gdn_a_log.astype(f32)) * jax.nn.softplus(a.astype(f32) + gdn_dt_bias.astype(f32))
    o, s_new = gated_delta_chunked(q, k, v, g, beta, s0.astype(f32))
    o = rms_norm(o, gdn_norm) * jax.nn.silu(z.astype(f32).reshape(bsz, t_len, GDN_HEADS, GDN_DV))
    y_a = o.reshape(bsz, t_len, V_W).astype(h.dtype) @ gdn_o
    glu_a, glu_b = jnp.split(glu, 2, axis=-1)
    cv = glu_a * jax.nn.sigmoid(glu_b)
    cv, conf_buf_new = causal_dwconv(cv, conf_buf, conf_conv_w)
    cv = layer_norm(cv + conf_conv_b, conf_ln_g, conf_ln_b)
    y_b = jax.nn.silu(cv) @ conf_pw2 + conf_pw2_b
    g_a, g_b = jnp.split(gates, 2, axis=-1)
    y = (jax.nn.sigmoid(g_a) * y_a + jax.nn.sigmoid(g_b) * y_b) @ w_out
    return y, gdn_buf_new.astype(gdn_buf.dtype), s_new.astype(s0.dtype), conf_buf_new.astype(conf_buf.dtype)


def mem_kv(mem, norm_mem, xa_kv):
    m = rms_norm(mem, norm_mem)
    kk, vv = jnp.split(m @ xa_kv, 2, axis=-1)
    shp = mem.shape[:2] + (XA_HEADS, XA_DH)
    return kk.reshape(shp), vv.reshape(shp)


def cross_attn(h, mk, mv, xa_q, xa_o):
    bsz, t_len, _ = h.shape
    q = (h @ xa_q).reshape(bsz, t_len, XA_HEADS, XA_DH)
    s = jnp.einsum("bthd,bmhd->bhtm", q, mk.astype(h.dtype)).astype(jnp.float32) * (XA_DH ** -0.5)
    p = jax.nn.softmax(s, axis=-1).astype(h.dtype)
    o = jnp.einsum("bhtm,bmhd->bthd", p, mv.astype(h.dtype)).reshape(bsz, t_len, XA_HEADS * XA_DH)
    return o @ xa_o


def clamped_swiglu(x_glu, x_lin):
    x_glu = jnp.minimum(x_glu, SWIGLU_LIMIT)
    x_lin = jnp.clip(x_lin, -SWIGLU_LIMIT, SWIGLU_LIMIT)
    return x_glu * jax.nn.sigmoid(SWIGLU_ALPHA * x_glu) * (x_lin + 1.0)


def moe(h, router_w, router_b, moe_w1, moe_b1, moe_w2, moe_b2):
    bsz, t_len, d = h.shape
    xt = h.reshape(-1, d)
    logits = (xt @ router_w + router_b).astype(jnp.float32)
    top_v, top_i = lax.top_k(logits, TOP_K)
    gate = jax.nn.softmax(top_v, axis=-1)
    combine = jnp.sum(jax.nn.one_hot(top_i, N_EXPERTS, dtype=jnp.float32) * gate[..., None], axis=1)
    out = jnp.zeros((xt.shape[0], d), jnp.float32)
    for e in range(N_EXPERTS):
        hid = xt @ moe_w1[e] + moe_b1[e]
        act = clamped_swiglu(hid[:, :D_FF], hid[:, D_FF:])
        out = out + combine[:, e:e + 1] * (act @ moe_w2[e] + moe_b2[e]).astype(jnp.float32)
    return out.astype(h.dtype).reshape(bsz, t_len, d)


def setup_inputs(seed: int = 0) -> dict:
    key = jax.random.key(seed)
    ks = iter(jax.random.split(key, 64))
    f32 = jnp.float32
    L = DEPTH

    def normal(shape, scale):
        return scale * jax.random.normal(next(ks), shape, f32)

    def gain(shape):
        return 1.0 + 0.02 * jax.random.normal(next(ks), shape, f32)

    inp = {}
    inp["x_prompt"] = normal((BATCH, SEQ, D_MODEL), 1.0)
    inp["x_sample"] = normal((DEC_BATCH, DEC_SEQ, D_MODEL), 1.0)
    inp["mem_prompt"] = normal((BATCH, MEM_LEN, D_MODEL), 1.0)
    inp["state_gdn"] = normal((L, DEC_BATCH, GDN_HEADS, GDN_DK, GDN_DV), 0.1)
    inp["state_gdn_conv"] = normal((L, DEC_BATCH, GDN_CONV - 1, QKV_W), 1.0)
    inp["state_conf_conv"] = normal((L, DEC_BATCH, CONF_WIDTH - 1, CONF_DIM), 1.0)
    inp["cache_mem_k"] = normal((L, DEC_BATCH, MEM_LEN, XA_HEADS, XA_DH), 1.0)
    inp["cache_mem_v"] = normal((L, DEC_BATCH, MEM_LEN, XA_HEADS, XA_DH), 1.0)
    inp["norm_mix"] = gain((L, D_MODEL))
    inp["w_in"] = normal((L, D_MODEL, N_IN), D_MODEL ** -0.5)
    inp["gdn_conv_w"] = normal((L, GDN_CONV, QKV_W), GDN_CONV ** -0.5)
    inp["gdn_a_log"] = jnp.log(jax.random.uniform(next(ks), (L, GDN_HEADS), f32, 1.0, 16.0))
    dt = jax.random.uniform(next(ks), (L, GDN_HEADS), f32, 0.001, 0.1)
    inp["gdn_dt_bias"] = dt + jnp.log(-jnp.expm1(-dt))
    inp["gdn_norm"] = gain((L, GDN_DV))
    inp["gdn_o"] = normal((L, V_W, D_MODEL), V_W ** -0.5)
    inp["conf_conv_w"] = normal((L, CONF_WIDTH, CONF_DIM), CONF_WIDTH ** -0.5)
    inp["conf_conv_b"] = normal((L, CONF_DIM), 0.02)
    inp["conf_ln_g"] = gain((L, CONF_DIM))
    inp["conf_ln_b"] = normal((L, CONF_DIM), 0.02)
    inp["conf_pw2"] = normal((L, CONF_DIM, D_MODEL), CONF_DIM ** -0.5)
    inp["conf_pw2_b"] = normal((L, D_MODEL), 0.02)
    inp["w_out"] = normal((L, D_MODEL, D_MODEL), D_MODEL ** -0.5)
    inp["norm_xa"] = gain((L, D_MODEL))
    inp["norm_mem"] = gain((L, D_MODEL))
    inp["xa_q"] = normal((L, D_MODEL, XA_HEADS * XA_DH), D_MODEL ** -0.5)
    inp["xa_kv"] = normal((L, D_MODEL, 2 * XA_HEADS * XA_DH), D_MODEL ** -0.5)
    inp["xa_o"] = normal((L, XA_HEADS * XA_DH, D_MODEL), (XA_HEADS * XA_DH) ** -0.5)
    inp["norm_ffn"] = gain((L, D_MODEL))
    inp["router_w"] = normal((L, D_MODEL, N_EXPERTS), D_MODEL ** -0.5)
    inp["router_b"] = normal((L, N_EXPERTS), 0.01)
    inp["moe_w1"] = normal((L, N_EXPERTS, D_MODEL, 2 * D_FF), D_MODEL ** -0.5)
    inp["moe_b1"] = normal((L, N_EXPERTS, 2 * D_FF), 0.01)
    inp["moe_w2"] = normal((L, N_EXPERTS, D_FF, D_MODEL), D_FF ** -0.5)
    inp["moe_b2"] = normal((L, N_EXPERTS, D_MODEL), 0.01)
    inp["norm_final"] = gain((D_MODEL,))
    return inp


def reference(x_prompt, x_sample, mem_prompt, state_gdn, state_gdn_conv, state_conf_conv, cache_mem_k, cache_mem_v,
              norm_mix, w_in, gdn_conv_w, gdn_a_log, gdn_dt_bias, gdn_norm, gdn_o, conf_conv_w, conf_conv_b,
              conf_ln_g, conf_ln_b, conf_pw2, conf_pw2_b, w_out, norm_xa, norm_mem, xa_q, xa_kv, xa_o,
              norm_ffn, router_w, router_b, moe_w1, moe_b1, moe_w2, moe_b2, norm_final):

    def block(x, mk, mv, gbuf, s, cbuf, l):
        h = rms_norm(x, norm_mix[l])
        y, gbuf, s, cbuf = mixer(h, gbuf, s, cbuf, w_in[l], gdn_conv_w[l], gdn_a_log[l], gdn_dt_bias[l],
                                 gdn_norm[l], gdn_o[l], conf_conv_w[l], conf_conv_b[l], conf_ln_g[l],
                                 conf_ln_b[l], conf_pw2[l], conf_pw2_b[l], w_out[l])
        x = x + y
        x = x + cross_attn(rms_norm(x, norm_xa[l]), mk, mv, xa_q[l], xa_o[l])
        x = x + moe(rms_norm(x, norm_ffn[l]), router_w[l], router_b[l], moe_w1[l], moe_b1[l], moe_w2[l], moe_b2[l])
        return x, gbuf, s, cbuf

    bp = x_prompt.shape[0]
    xp, xs = x_prompt, x_sample
    p_gdn, p_gconv, p_cconv, p_mk, p_mv = [], [], [], [], []
    s_gdn, s_gconv, s_cconv = [], [], []
    for l in range(DEPTH):
        mk_p, mv_p = mem_kv(mem_prompt, norm_mem[l], xa_kv[l])
        gbuf0 = jnp.zeros((bp, GDN_CONV - 1, QKV_W), xp.dtype)
        s0 = jnp.zeros((bp, GDN_HEADS, GDN_DK, GDN_DV), xp.dtype)
        cbuf0 = jnp.zeros((bp, CONF_WIDTH - 1, CONF_DIM), xp.dtype)
        xp, gp, sp, cp = block(xp, mk_p, mv_p, gbuf0, s0, cbuf0, l)
        p_gdn.append(sp); p_gconv.append(gp); p_cconv.append(cp); p_mk.append(mk_p); p_mv.append(mv_p)
        xs, gs, ss, cs = block(xs, cache_mem_k[l], cache_mem_v[l], state_gdn_conv[l], state_gdn[l],
                               state_conf_conv[l], l)
        s_gdn.append(ss); s_gconv.append(gs); s_cconv.append(cs)
    y_prompt = rms_norm(xp, norm_final)
    y_sample = rms_norm(xs, norm_final)
    return (y_prompt, y_sample, jnp.stack(p_gdn), jnp.stack(p_gconv), jnp.stack(p_cconv), jnp.stack(p_mk),
            jnp.stack(p_mv), jnp.stack(s_gdn), jnp.stack(s_gconv), jnp.stack(s_cconv))
```

```python
import functools

import jax
import jax.numpy as jnp
from jax import lax
from jax.experimental import pallas as pl
from jax.experimental.pallas import tpu as pltpu

F32, BF16, I32 = jnp.float32, jnp.bfloat16, jnp.int32
HIGHEST = lax.Precision.HIGHEST

LANES = 128
V7X_VMEM_REQUEST = 56 * 1024 * 1024

TOKEN_TILE = 512
GDN_CHUNK = 64
GDN_GROUP = 256
GDN_HEADS_PER_STEP = 4
SEQ_BLOCK = 16
CONF_HALO = 32
CONF_ROWS = 16
MOE_ROW_TILE = 256
CUMSUM_CHUNK = 512

TOP_K = 4
SWIGLU_ALPHA = 1.702
SWIGLU_LIMIT = 7.0
RMS_EPS = 1e-6
LN_EPS = 1e-5
L2_EPS = 1e-6


def _params(n_axes):
    return pltpu.CompilerParams(dimension_semantics=("arbitrary",) * n_axes, vmem_limit_bytes=V7X_VMEM_REQUEST)


def _const_spec(shape):
    zeros = (0,) * len(shape)
    return pl.BlockSpec(shape, lambda *_: zeros, pipeline_mode=pl.Buffered(1))


def _mm(a, b):
    return jnp.dot(a.astype(BF16), b.astype(BF16), preferred_element_type=F32)


def _mm_nt(a, b):
    return lax.dot_general(a.astype(BF16), b.astype(BF16), (((1,), (1,)), ((), ())), preferred_element_type=F32)


def _rms(x, gain):
    return x * lax.rsqrt(jnp.mean(x * x, axis=-1, keepdims=True) + RMS_EPS) * gain


def _silu(x):
    return x * jax.nn.sigmoid(x)


def _softplus(x):
    return jnp.maximum(x, 0.0) + jnp.log1p(jnp.exp(-jnp.abs(x)))


def _split_hi_lo(x):
    hi = x.astype(BF16)
    lo = (x - hi.astype(F32)).astype(BF16)
    return hi, lo


def _inproj_body(xp_ref, xs_ref, gain_ref, wqkvz_ref, wab_ref, wglu_ref, wgate_ref,
                 qkv_ref, z_ref, ab_ref, cv_ref, sg_ref, *, n_prompt_tiles):
    i = pl.program_id(0)
    d = xp_ref.shape[1]
    x = jnp.where(i < n_prompt_tiles, xp_ref[...], xs_ref[...])
    h = _rms(x, gain_ref[...]).astype(BF16)
    qkv_w = qkv_ref.shape[1]
    for c in range(0, qkv_w, d):
        qkv_ref[:, c:c + d] = jnp.dot(h, wqkvz_ref[:, c:c + d], preferred_element_type=F32).astype(BF16)
    z_ref[...] = jnp.dot(h, wqkvz_ref[:, qkv_w:], preferred_element_type=F32).astype(BF16)
    ab_ref[...] = jnp.dot(h, wab_ref[...], preferred_element_type=F32)
    glu_a = jnp.dot(h, wglu_ref[:, :d], preferred_element_type=F32)
    glu_b = jnp.dot(h, wglu_ref[:, d:], preferred_element_type=F32)
    cv_ref[...] = glu_a * jax.nn.sigmoid(glu_b)
    for c in range(0, 2 * d, d):
        sg_ref[:, c:c + d] = jax.nn.sigmoid(
            jnp.dot(h, wgate_ref[:, c:c + d], preferred_element_type=F32)).astype(BF16)


def _inproj(xp, xs, gain, wqkvz, wab, wglu, wgate, qkv_w):
    n_p, d = xp.shape
    n_s = xs.shape[0]
    tm = TOKEN_TILE
    npt, nst = n_p // tm, n_s // tm
    n = n_p + n_s
    row = lambda i: (i, 0)
    return pl.pallas_call(
        functools.partial(_inproj_body, n_prompt_tiles=npt),
        grid=(npt + nst,),
        in_specs=[
            pl.BlockSpec((tm, d), lambda i: (jnp.minimum(i, npt - 1), 0)),
            pl.BlockSpec((tm, d), lambda i: (jnp.maximum(i - npt, 0), 0)),
            _const_spec(gain.shape), _const_spec(wqkvz.shape), _const_spec(wab.shape),
            _const_spec(wglu.shape), _const_spec(wgate.shape),
        ],
        out_specs=[
            pl.BlockSpec((tm, qkv_w), row), pl.BlockSpec((tm, d), row), pl.BlockSpec((tm, LANES), row),
            pl.BlockSpec((tm, d), row), pl.BlockSpec((tm, 2 * d), row),
        ],
        out_shape=[
            jax.ShapeDtypeStruct((n, qkv_w), BF16), jax.ShapeDtypeStruct((n, d), BF16),
            jax.ShapeDtypeStruct((n, LANES), F32), jax.ShapeDtypeStruct((n, d), F32),
            jax.ShapeDtypeStruct((n, 2 * d), BF16),
        ],
        compiler_params=_params(1),
        name="inproj",
    )(xp, xs, gain, wqkvz, wab, wglu, wgate)


def _head_scalars(ab, head, n_heads, alog_ref, dtb_ref):
    rows = lax.broadcasted_iota(I32, (LANES, 2 * LANES), 0)
    cols = lax.broadcasted_iota(I32, (LANES, 2 * LANES), 1)
    pick = jnp.where(cols < LANES, head, head + n_heads)
    expand = (rows == pick).astype(F32)
    both = jnp.dot(ab, expand, precision=HIGHEST, preferred_element_type=F32)
    a, b = both[:, :LANES], both[:, LANES:]
    a_log = alog_ref[pl.ds(head, 1), :]
    dt_bias = dtb_ref[pl.ds(head, 1), :]
    g = -jnp.exp(a_log) * _softplus(a + dt_bias)
    return g, jax.nn.sigmoid(b)


def _cumsum_rows(incl_bf16, g):
    hi, lo = _split_hi_lo(g)
    both = jnp.dot(incl_bf16, jnp.concatenate([hi, lo], axis=1), preferred_element_type=F32)
    return both[:, :LANES] + both[:, LANES:]


def _l2norm(x):
    return x * lax.rsqrt(jnp.sum(x * x, axis=-1, keepdims=True) + L2_EPS)


def _wy_solve(q, k, v, g, beta, incl, strict, n_doublings):
    r = q.shape[0]
    gc = _cumsum_rows(incl.astype(BF16), g)
    gc_wide = jnp.concatenate([gc] * (r // LANES), axis=1) if r > LANES else gc[:, :r]
    gc_t = jnp.transpose(jnp.concatenate([gc, jnp.zeros((max(LANES - r, 0), LANES), F32)], axis=0)
                         if r < LANES else gc)
    gc_row = gc_t[0:1, :r]
    decay = jnp.where(incl, jnp.exp(jnp.where(incl, gc_wide - gc_row, 0.0)), 0.0)
    kb = k * beta
    gram = _mm_nt(jnp.concatenate([kb, q], axis=0), k)
    neg_m = jnp.where(strict, -(gram[:r] * decay), 0.0)
    p = gram[r:] * decay
    eye = (lax.broadcasted_iota(I32, (r, r), 0) == lax.broadcasted_iota(I32, (r, r), 1)).astype(F32)
    t_inv = eye + neg_m
    power = neg_m
    for _ in range(n_doublings):
        power = _mm(power, power)
        t_inv = t_inv + _mm(t_inv, power)
    sol = _mm(t_inv, jnp.concatenate([v * beta, kb * jnp.exp(gc)], axis=1))
    return gc, sol[:, :LANES], sol[:, LANES:], p


def _gated_out(o, gnorm, z):
    on = o * lax.rsqrt(jnp.mean(o * o, axis=-1, keepdims=True) + RMS_EPS) * gnorm
    return (on * _silu(z)).astype(BF16)


def _gdn_prompt_body(q_ref, k_ref, v_ref, z_ref, ab_ref, cwq_ref, cwk_ref, cwv_ref, alog_ref, dtb_ref,
                     gnorm_ref, og_ref, sout_ref, halo_ref, s_ref, *, n_heads, n_taps):
    hg = pl.program_id(1)
    heads_here = s_ref.shape[0]
    dk = s_ref.shape[1]
    grp, chunk = GDN_GROUP, GDN_CHUNK
    n_groups = q_ref.shape[0] // grp
    halo_rows = halo_ref.shape[1]
    halo_ref[...] = jnp.zeros(halo_ref.shape, F32)
    s_ref[...] = jnp.zeros(s_ref.shape, F32)

    ri = lax.broadcasted_iota(I32, (grp, grp), 0)
    ci = lax.broadcasted_iota(I32, (grp, grp), 1)
    same = (ri // chunk) == (ci // chunk)
    incl = same & (ri >= ci)
    strict = same & (ri > ci)

    def group(gi, carry):
        rows = pl.ds(pl.multiple_of(gi * grp, grp), grp)
        ab = ab_ref[rows, :]
        conv = []
        for part, (x_ref, cw_ref) in enumerate(((q_ref, cwq_ref), (k_ref, cwk_ref), (v_ref, cwv_ref))):
            xg = x_ref[rows, :].astype(F32)
            xw = jnp.concatenate([halo_ref[part], xg], axis=0)
            halo_ref[part] = xg[grp - halo_rows:, :]
            y = None
            for j in range(n_taps):
                off = halo_rows - (n_taps - 1) + j
                term = cw_ref[j:j + 1, :] * xw[off:off + grp, :]
                y = term if y is None else y + term
            conv.append(_silu(y))
        for hh in range(heads_here):
            sl = slice(hh * dk, (hh + 1) * dk)
            head = hg * heads_here + hh
            q = _l2norm(conv[0][:, sl]) * (dk ** -0.5)
            k = _l2norm(conv[1][:, sl])
            v = conv[2][:, sl]
            g, beta = _head_scalars(ab, head, n_heads, alog_ref, dtb_ref)
            gc, u, w, p = _wy_solve(q, k, v, g, beta, incl, strict, n_doublings=5)
            qe = q * jnp.exp(gc)
            state = s_ref[hh]
            outs = []
            for j in range(grp // chunk):
                cs = slice(j * chunk, (j + 1) * chunk)
                ws_qs = _mm(jnp.concatenate([w[cs], qe[cs]], axis=0), state)
                v_new = u[cs] - ws_qs[:chunk]
                outs.append(ws_qs[chunk:] + _mm(p[cs, cs], v_new))
                g_last = gc[(j + 1) * chunk - 1:(j + 1) * chunk, :]
                k_dec = k[cs] * jnp.exp(g_last - gc[cs])
                state = state * jnp.exp(g_last) + lax.dot_general(
                    k_dec.astype(BF16), v_new.astype(BF16), (((0,), (0,)), ((), ())), preferred_element_type=F32)
            s_ref[hh] = state
            o = jnp.concatenate(outs, axis=0)
            og_ref[rows, sl] = _gated_out(o, gnorm_ref[...], z_ref[rows, sl].astype(F32))
        return carry

    lax.fori_loop(0, n_groups, group, 0)
    sout_ref[0] = s_ref[...]


def _gdn_prompt(qkv, z, ab, conv_w, alog, dtb, gnorm, n_batch, seq, n_heads, dk):
    hps = GDN_HEADS_PER_STEP
    hgw = hps * dk
    n_hg = n_heads * dk // hgw
    n_taps = conv_w.shape[0]
    col = lambda part: (lambda b, h: (b, part * n_hg + h))
    cwcol = lambda part: (lambda b, h: (0, part * n_hg + h))
    return pl.pallas_call(
        functools.partial(_gdn_prompt_body, n_heads=n_heads, n_taps=n_taps),
        grid=(n_batch, n_hg),
        in_specs=[
            pl.BlockSpec((seq, hgw), col(0)), pl.BlockSpec((seq, hgw), col(1)), pl.BlockSpec((seq, hgw), col(2)),
            pl.BlockSpec((seq, hgw), lambda b, h: (b, h)),
            pl.BlockSpec((seq, LANES), lambda b, h: (b, 0)),
            pl.BlockSpec((n_taps, hgw), cwcol(0)), pl.BlockSpec((n_taps, hgw), cwcol(1)),
            pl.BlockSpec((n_taps, hgw), cwcol(2)),
            _const_spec(alog.shape), _const_spec(dtb.shape), _const_spec(gnorm.shape),
        ],
        out_specs=[
            pl.BlockSpec((seq, hgw), lambda b, h: (b, h)),
            pl.BlockSpec((1, hps, dk, dk), lambda b, h: (b, h, 0, 0)),
        ],
        out_shape=[
            jax.ShapeDtypeStruct((n_batch * seq, n_heads * dk), BF16),
            jax.ShapeDtypeStruct((n_batch, n_heads, dk, dk), F32),
        ],
        scratch_shapes=[pltpu.VMEM((3, 8, hgw), F32), pltpu.VMEM((hps, dk, dk), F32)],
        compiler_params=_params(2),
        name="gdn_prompt",
    )(qkv, qkv, qkv, z, ab, conv_w, conv_w, conv_w, alog, dtb, gnorm)


def _gdn_sample_body(*refs, n_heads, n_tok, n_taps):
    sb = SEQ_BLOCK
    it = iter(refs)
    take = lambda n: [next(it) for _ in range(n)]
    x_refs = [take(n_tok) for _ in range(3)]
    cb_refs = take(3)
    z_refs = take(n_tok)
    ab_refs = take(n_tok)
    s0_ref = next(it)
    cw_refs = take(3)
    alog_ref, dtb_ref, gnorm_ref = take(3)
    og_refs = take(n_tok)
    sout_ref = next(it)

    hg = pl.program_id(1)
    heads_here, dk = s0_ref.shape[1], s0_ref.shape[2]
    r = n_tok * sb

    conv = []
    for part in range(3):
        ext = [cb_refs[part][:, j, :] for j in range(n_taps - 1)]
        ext += [x_refs[part][t][...].astype(F32) for t in range(n_tok)]
        ys = []
        for t in range(n_tok):
            y = None
            for j in range(n_taps):
                term = cw_refs[part][j:j + 1, :] * ext[t + j]
                y = term if y is None else y + term
            ys.append(_silu(y))
        conv.append(jnp.concatenate(ys, axis=0))
    ab = jnp.concatenate([a[...] for a in ab_refs], axis=0)

    ri = lax.broadcasted_iota(I32, (r, r), 0)
    ci = lax.broadcasted_iota(I32, (r, r), 1)
    same = (ri % sb) == (ci % sb)
    incl = same & (ri >= ci)
    strict = same & (ri > ci)
    row_seq = lax.broadcasted_iota(I32, (2 * r, dk), 0) % sb
    col_seq = lax.broadcasted_iota(I32, (dk, LANES), 1) % sb
    n_doublings = max((n_tok - 1).bit_length() - 1, 0)

    for hh in range(heads_here):
        sl = slice(hh * dk, (hh + 1) * dk)
        head = hg * heads_here + hh
        q = _l2norm(conv[0][:, sl]) * (dk ** -0.5)
        k = _l2norm(conv[1][:, sl])
        v = conv[2][:, sl]
        g, beta = _head_scalars(ab, head, n_heads, alog_ref, dtb_ref)
        gc, u, w, p = _wy_solve(q, k, v, g, beta, incl, strict, n_doublings=n_doublings)
        wq = jnp.concatenate([w, q * jnp.exp(gc)], axis=0).astype(BF16)
        ws_qs = jnp.zeros((2 * r, dk), F32)
        for s in range(sb):
            ws_qs = jnp.where(row_seq == s, _mm(wq, s0_ref[s, hh]), ws_qs)
        v_new = u - ws_qs[:r]
        o = ws_qs[r:] + _mm(p, v_new)
        z = jnp.concatenate([zr[:, sl] for zr in z_refs], axis=0).astype(F32)
        og = _gated_out(o, gnorm_ref[...], z)
        for t in range(n_tok):
            og_refs[t][:, sl] = og[t * sb:(t + 1) * sb]
        g_end = gc[r - sb:, :]
        k_dec = k * jnp.exp(jnp.concatenate([g_end] * n_tok, axis=0) - gc)
        k_dec_t = jnp.transpose(jnp.concatenate([k_dec, jnp.zeros((LANES - r, dk), F32)], axis=0))
        v_new_pad = jnp.concatenate([v_new, jnp.zeros((LANES - r, dk), F32)], axis=0).astype(BF16)
        for s in range(sb):
            upd = _mm(jnp.where(col_seq == s, k_dec_t, 0.0), v_new_pad)
            sout_ref[s, hh] = s0_ref[s, hh] * jnp.exp(g_end[s:s + 1, :]) + upd


def _gdn_sample(qkv, z, ab, conv_buf, s0, conv_w, alog, dtb, gnorm, n_prompt_rows, n_seq, n_tok, n_heads, dk):
    sb, hps = SEQ_BLOCK, GDN_HEADS_PER_STEP
    hgw = hps * dk
    n_hg = n_heads * dk // hgw
    n_taps = conv_w.shape[0]
    base = n_prompt_rows // sb
    per_t = n_seq // sb

    def tok_rows(t, colfn):
        return lambda i, h: (base + t * per_t + i, colfn(h))

    in_specs, args = [], []
    for part in range(3):
        for t in range(n_tok):
            in_specs.append(pl.BlockSpec((sb, hgw), tok_rows(t, lambda h, part=part: part * n_hg + h)))
            args.append(qkv)
    for part in range(3):
        in_specs.append(pl.BlockSpec((sb, n_taps - 1, hgw), lambda i, h, part=part: (i, 0, part * n_hg + h)))
        args.append(conv_buf)
    for t in range(n_tok):
        in_specs.append(pl.BlockSpec((sb, hgw), tok_rows(t, lambda h: h)))
        args.append(z)
    for t in range(n_tok):
        in_specs.append(pl.BlockSpec((sb, LANES), tok_rows(t, lambda h: 0)))
        args.append(ab)
    in_specs.append(pl.BlockSpec((sb, hps, dk, dk), lambda i, h: (i, h, 0, 0)))
    args.append(s0)
    for part in range(3):
        in_specs.append(pl.BlockSpec((n_taps, hgw), lambda i, h, part=part: (0, part * n_hg + h)))
        args.append(conv_w)
    in_specs += [_const_spec(alog.shape), _const_spec(dtb.shape), _const_spec(gnorm.shape)]
    args += [alog, dtb, gnorm]
    outs = pl.pallas_call(
        functools.partial(_gdn_sample_body, n_heads=n_heads, n_tok=n_tok, n_taps=n_taps),
        grid=(per_t, n_hg),
        in_specs=in_specs,
        out_specs=[pl.BlockSpec((sb, hgw), lambda i, h: (i, h))] * n_tok
        + [pl.BlockSpec((sb, hps, dk, dk), lambda i, h: (i, h, 0, 0))],
        out_shape=[jax.ShapeDtypeStruct((n_seq, n_heads * dk), BF16)] * n_tok
        + [jax.ShapeDtypeStruct(s0.shape, F32)],
        compiler_params=_params(2),
        name="gdn_sample",
    )(*args)
    return jnp.concatenate(outs[:n_tok], axis=0), outs[n_tok]


def _ln_silu(y, lg, lb):
    mu = jnp.mean(y, axis=-1, keepdims=True)
    yc = y - mu
    var = jnp.mean(yc * yc, axis=-1, keepdims=True)
    return _silu(yc * lax.rsqrt(var + LN_EPS) * lg + lb).astype(BF16)


def _conf_prompt_body(cv_ref, halo_ref, cw_ref, cb_ref, lg_ref, lb_ref, out_ref, rot_ref, *, n_taps):
    tile_in_seq = pl.program_id(1)
    tt, c = cv_ref.shape
    halo = CONF_HALO
    rot_ref[0, 0:halo, :] = jnp.where(tile_in_seq == 0, 0.0, halo_ref[...])
    rot_ref[0, halo:halo + tt, :] = cv_ref[...]
    rot_ref[0, halo + tt:halo + tt + 8, :] = jnp.zeros((8, c), F32)
    step = 32

    def rotate(i, carry):
        base = pl.multiple_of(i * step, step)
        win = rot_ref[0, pl.ds(base, step + 8), :]
        for r in range(1, 8):
            rot_ref[r, pl.ds(base, step), :] = win[r:r + step, :]
        return carry

    lax.fori_loop(0, (halo + tt) // step, rotate, 0)
    lead = halo - (n_taps - 1)
    rows = CONF_ROWS

    def conv(i, carry):
        base = pl.multiple_of(i * rows, rows)
        acc = jnp.broadcast_to(cb_ref[...], (rows, c))
        for j in range(n_taps):
            off = lead + j
            start = pl.multiple_of(base + (off // 8) * 8, 8)
            acc = acc + cw_ref[j:j + 1, :] * rot_ref[off % 8, pl.ds(start, rows), :]
        out_ref[pl.ds(base, rows), :] = _ln_silu(acc, lg_ref[...], lb_ref[...])
        return carry

    lax.fori_loop(0, tt // rows, conv, 0)


def _conf_prompt(cv, cw, cb, lg, lb, n_batch, seq):
    c = cv.shape[1]
    tt = TOKEN_TILE
    tiles = seq // tt
    halo_per_tile = tt // CONF_HALO
    n_taps = cw.shape[0]
    return pl.pallas_call(
        functools.partial(_conf_prompt_body, n_taps=n_taps),
        grid=(n_batch, tiles),
        in_specs=[
            pl.BlockSpec((tt, c), lambda b, j: (b * tiles + j, 0)),
            pl.BlockSpec((CONF_HALO, c), lambda b, j: (jnp.maximum((b * tiles + j) * halo_per_tile - 1, 0), 0)),
            _const_spec(cw.shape), _const_spec(cb.shape), _const_spec(lg.shape), _const_spec(lb.shape),
        ],
        out_specs=pl.BlockSpec((tt, c), lambda b, j: (b * tiles + j, 0)),
        out_shape=jax.ShapeDtypeStruct((n_batch * seq, c), BF16),
        scratch_shapes=[pltpu.VMEM((8, CONF_HALO + tt + 8, c), F32)],
        compiler_params=_params(2),
        name="conf_prompt",
    )(cv, cv, cw, cb, lg, lb)


def _conf_sample_body(*refs, n_tok, n_taps):
    buf_ref = refs[0]
    cv_refs = refs[1:1 + n_tok]
    cw_ref, cb_ref, lg_ref, lb_ref = refs[1 + n_tok:5 + n_tok]
    out_refs = refs[5 + n_tok:]
    n_hist = n_taps - 1

    def ext(i):
        return buf_ref[:, i, :] if i < n_hist else cv_refs[i - n_hist][...]

    for t in range(n_tok):
        acc = jnp.broadcast_to(cb_ref[...], cv_refs[0].shape)
        for j in range(n_taps):
            acc = acc + cw_ref[j:j + 1, :] * ext(t + j)
        out_refs[t][...] = _ln_silu(acc, lg_ref[...], lb_ref[...])


def _conf_sample(cv, buf, cw, cb, lg, lb, n_prompt_rows, n_seq, n_tok):
    sb = SEQ_BLOCK
    c = cv.shape[1]
    n_taps = cw.shape[0]
    base, per_t = n_prompt_rows // sb, n_seq // sb
    in_specs = [pl.BlockSpec((sb, n_taps - 1, c), lambda i: (i, 0, 0))]
    in_specs += [pl.BlockSpec((sb, c), lambda i, t=t: (base + t * per_t + i, 0)) for t in range(n_tok)]
    in_specs += [_const_spec(cw.shape), _const_spec(cb.shape), _const_spec(lg.shape), _const_spec(lb.shape)]
    outs = pl.pallas_call(
        functools.partial(_conf_sample_body, n_tok=n_tok, n_taps=n_taps),
        grid=(per_t,),
        in_specs=in_specs,
        out_specs=[pl.BlockSpec((sb, c), lambda i: (i, 0))] * n_tok,
        out_shape=[jax.ShapeDtypeStruct((n_seq, c), BF16)] * n_tok,
        compiler_params=_params(1),
        name="conf_sample",
    )(buf, *([cv] * n_tok), cw, cb, lg, lb)
    return jnp.concatenate(outs, axis=0)


def _merge_body(ogp_ref, ogs_ref, ybp_ref, ybs_ref, sg_ref, xp_ref, xs_ref, wgo_ref, wpw_ref, bpw_ref,
                wout_ref, gxa_ref, wq_ref, x1_ref, q_ref, *, n_prompt_tiles):
    is_prompt = pl.program_id(0) < n_prompt_tiles
    d = xp_ref.shape[1]
    og = jnp.where(is_prompt, ogp_ref[...], ogs_ref[...])
    yb_in = jnp.where(is_prompt, ybp_ref[...], ybs_ref[...])
    x = jnp.where(is_prompt, xp_ref[...], xs_ref[...])
    y_a = jnp.dot(og, wgo_ref[...], preferred_element_type=F32)
    y_b = jnp.dot(yb_in, wpw_ref[...], preferred_element_type=F32) + bpw_ref[...]
    mixed = sg_ref[:, :d].astype(F32) * y_a + sg_ref[:, d:].astype(F32) * y_b
    x1 = x + jnp.dot(mixed.astype(BF16), wout_ref[...], preferred_element_type=F32)
    x1_ref[...] = x1
    q_ref[...] = jnp.dot(_rms(x1, gxa_ref[...]).astype(BF16), wq_ref[...],
                         preferred_element_type=F32).astype(BF16)


def _dual(tm, cols, npt):
    return [pl.BlockSpec((tm, cols), lambda i: (jnp.minimum(i, npt - 1), 0)),
            pl.BlockSpec((tm, cols), lambda i: (jnp.maximum(i - npt, 0), 0))]


def _merge(og_p, og_s, yb_p, yb_s, sg, xp, xs, wgo, wpw, bpw, wout, gxa, wq):
    n_p, d = xp.shape
    tm = TOKEN_TILE
    npt, nst = n_p // tm, xs.shape[0] // tm
    n = n_p + xs.shape[0]
    row = lambda i: (i, 0)
    return pl.pallas_call(
        functools.partial(_merge_body, n_prompt_tiles=npt),
        grid=(npt + nst,),
        in_specs=_dual(tm, d, npt) + _dual(tm, d, npt) + [pl.BlockSpec((tm, 2 * d), row)] + _dual(tm, d, npt)
        + [_const_spec(a.shape) for a in (wgo, wpw, bpw, wout, gxa, wq)],
        out_specs=[pl.BlockSpec((tm, d), row), pl.BlockSpec((tm, d), row)],
        out_shape=[jax.ShapeDtypeStruct((n, d), F32), jax.ShapeDtypeStruct((n, d), BF16)],
        compiler_params=_params(1),
        name="merge",
    )(og_p, og_s, yb_p, yb_s, sg, xp, xs, wgo, wpw, bpw, wout, gxa, wq)


def _memkv_body(m_ref, gain_ref, w_ref, k_ref, v_ref):
    d = k_ref.shape[1]
    h = _rms(m_ref[...], gain_ref[...]).astype(BF16)
    k_ref[...] = jnp.dot(h, w_ref[:, :d], preferred_element_type=F32)
    v_ref[...] = jnp.dot(h, w_ref[:, d:], preferred_element_type=F32)


def _memkv(mem, gain, w):
    n, d = mem.shape
    tm = min(TOKEN_TILE, n)
    row = lambda i: (i, 0)
    return pl.pallas_call(
        _memkv_body,
        grid=(n // tm,),
        in_specs=[pl.BlockSpec((tm, d), row), _const_spec(gain.shape), _const_spec(w.shape)],
        out_specs=[pl.BlockSpec((tm, d), row)] * 2,
        out_shape=[jax.ShapeDtypeStruct((n, d), F32)] * 2,
        compiler_params=_params(1),
        name="memkv",
    )(mem, gain, w)


def _softmax_rows(s):
    e = jnp.exp(s - jnp.max(s, axis=-1, keepdims=True))
    return e / jnp.sum(e, axis=-1, keepdims=True)


def _attn_prompt_body(q_ref, k_ref, v_ref, o_ref, *, n_heads):
    dh = q_ref.shape[1] // n_heads
    for h in range(n_heads):
        sl = slice(h * dh, (h + 1) * dh)
        s = _mm_nt(q_ref[:, sl], k_ref[:, sl]) * (dh ** -0.5)
        o_ref[:, sl] = _mm(_softmax_rows(s), v_ref[:, sl]).astype(BF16)


def _attn_prompt(q, k, v, n_batch, seq, mem_len, n_heads):
    d = q.shape[1]
    tq = TOKEN_TILE
    tiles = seq // tq
    return pl.pallas_call(
        functools.partial(_attn_prompt_body, n_heads=n_heads),
        grid=(n_batch, tiles),
        in_specs=[
            pl.BlockSpec((tq, d), lambda b, j: (b * tiles + j, 0)),
            pl.BlockSpec((mem_len, d), lambda b, j: (b, 0)),
            pl.BlockSpec((mem_len, d), lambda b, j: (b, 0)),
        ],
        out_specs=pl.BlockSpec((tq, d), lambda b, j: (b * tiles + j, 0)),
        out_shape=jax.ShapeDtypeStruct((n_batch * seq, d), BF16),
        compiler_params=_params(2),
        name="attn_prompt",
    )(q, k, v)


def _attn_sample_body(*refs, n_tok):
    sb = SEQ_BLOCK
    q_refs = refs[:n_tok]
    k_ref, v_ref = refs[n_tok], refs[n_tok + 1]
    o_refs = refs[n_tok + 2:]
    _, mem_len, dh = k_ref.shape
    q = jnp.concatenate([qr[...] for qr in q_refs], axis=0)
    keys = k_ref[...].reshape(sb * mem_len, dh).astype(BF16)
    vals = v_ref[...].reshape(sb * mem_len, dh).astype(BF16)
    s = _mm_nt(q, keys) * (dh ** -0.5)
    row_seq = lax.broadcasted_iota(I32, s.shape, 0) % sb
    col_seq = lax.broadcasted_iota(I32, s.shape, 1) // mem_len
    p = _softmax_rows(jnp.where(row_seq == col_seq, s, -jnp.inf))
    o = _mm(p, vals).astype(BF16)
    for t in range(n_tok):
        o_refs[t][...] = o[t * sb:(t + 1) * sb]


def _attn_sample(q, kc, vc, n_prompt_rows, n_seq, n_tok, n_heads):
    sb = SEQ_BLOCK
    d = q.shape[1]
    dh = d // n_heads
    mem_len = kc.shape[1]
    base, per_t = n_prompt_rows // sb, n_seq // sb
    in_specs = [pl.BlockSpec((sb, dh), lambda i, h, t=t: (base + t * per_t + i, h)) for t in range(n_tok)]
    in_specs += [pl.BlockSpec((sb, mem_len, dh), lambda i, h: (i, 0, h))] * 2
    outs = pl.pallas_call(
        functools.partial(_attn_sample_body, n_tok=n_tok),
        grid=(per_t, n_heads),
        in_specs=in_specs,
        out_specs=[pl.BlockSpec((sb, dh), lambda i, h: (i, h))] * n_tok,
        out_shape=[jax.ShapeDtypeStruct((n_seq, d), BF16)] * n_tok,
        compiler_params=_params(2),
        name="attn_sample",
    )(*([q] * n_tok), kc, vc)
    return jnp.concatenate(outs, axis=0)


def _router_body(op_ref, os_ref, x1_ref, wo_ref, gain_ref, rwt_ref, rb_ref,
                 x2_ref, hf_ref, idx_ref, gate_ref, *, n_prompt_tiles):
    o = jnp.where(pl.program_id(0) < n_prompt_tiles, op_ref[...], os_ref[...])
    x2 = x1_ref[...] + jnp.dot(o, wo_ref[...], preferred_element_type=F32)
    x2_ref[...] = x2
    hf = _rms(x2, gain_ref[...])
    hf_ref[...] = hf
    logits = lax.dot_general(rwt_ref[...], hf, (((1,), (1,)), ((), ())), precision=HIGHEST,
                             preferred_element_type=F32) + rb_ref[...]
    n_exp, tm = logits.shape
    expert = lax.broadcasted_iota(I32, logits.shape, 0)
    vals, picks = [], []
    for _ in range(TOP_K):
        best = jnp.max(logits, axis=0, keepdims=True)
        pick = jnp.min(jnp.where(logits == best, expert, n_exp), axis=0, keepdims=True)
        vals.append(best)
        picks.append(pick)
        logits = jnp.where(expert == pick, -jnp.inf, logits)
    idx_ref[...] = jnp.concatenate(picks, axis=0)
    es = [jnp.exp(v - vals[0]) for v in vals]
    total = functools.reduce(lambda a, b: a + b, es)
    gates = jnp.concatenate([e / total for e in es] + [jnp.zeros((LANES - TOP_K, tm), F32)], axis=0)
    gate_ref[...] = jnp.transpose(gates)


def _router(o_p, o_s, x1, wo, gain, rwt, rb):
    n, d = x1.shape
    tm = TOKEN_TILE
    npt = o_p.shape[0] // tm
    row = lambda i: (i, 0)
    return pl.pallas_call(
        functools.partial(_router_body, n_prompt_tiles=npt),
        grid=(n // tm,),
        in_specs=_dual(tm, d, npt) + [pl.BlockSpec((tm, d), row)]
        + [_const_spec(a.shape) for a in (wo, gain, rwt, rb)],
        out_specs=[pl.BlockSpec((tm, d), row), pl.BlockSpec((tm, d), row),
                   pl.BlockSpec((TOP_K, tm), lambda i: (0, i)), pl.BlockSpec((tm, LANES), row)],
        out_shape=[jax.ShapeDtypeStruct((n, d), F32), jax.ShapeDtypeStruct((n, d), F32),
                   jax.ShapeDtypeStruct((TOP_K, n), I32), jax.ShapeDtypeStruct((n, LANES), F32)],
        compiler_params=_params(1),
        name="router",
    )(o_p, o_s, x1, wo, gain, rwt, rb)


def _positions_body(idx_ref, pos_ref, counts_ref, rank_ref, *, n_experts):
    ch = CUMSUM_CHUNK
    n = idx_ref.shape[1]
    n_chunks = n // ch
    upper = (lax.broadcasted_iota(I32, (ch, ch), 0) <= lax.broadcasted_iota(I32, (ch, ch), 1)).astype(BF16)
    expert = lax.broadcasted_iota(I32, (n_experts, ch), 0)

    def onehot(k, c):
        cols = pl.ds(pl.multiple_of(c * ch, ch), ch)
        return cols, expert == idx_ref[k:k + 1, cols]

    running = jnp.zeros((n_experts, 1), F32)
    for k in range(TOP_K):
        def count(c, run, k=k):
            cols, oh = onehot(k, c)
            cum = jnp.dot(oh.astype(BF16), upper, preferred_element_type=F32)
            rank_ref[k:k + 1, cols] = jnp.sum(jnp.where(oh, cum - 1.0 + run, 0.0), axis=0, keepdims=True)
            return run + cum[:, ch - 1:ch]
        running = lax.fori_loop(0, n_chunks, count, running)

    counts = jnp.broadcast_to(running, (n_experts, LANES))
    counts_ref[...] = counts.astype(I32)
    lower = (lax.broadcasted_iota(I32, (n_experts, n_experts), 0)
             > lax.broadcasted_iota(I32, (n_experts, n_experts), 1)).astype(F32)
    start = jnp.dot(lower, counts, precision=HIGHEST, preferred_element_type=F32)[:, 0:1]

    for k in range(TOP_K):
        def place(c, carry, k=k):
            cols, oh = onehot(k, c)
            first = jnp.sum(jnp.where(oh, start, 0.0), axis=0, keepdims=True)
            pos_ref[k:k + 1, cols] = (rank_ref[k:k + 1, cols] + first).astype(I32)
            return carry
        lax.fori_loop(0, n_chunks, place, 0)


def _positions(idx, n_experts):
    k, n = idx.shape
    return pl.pallas_call(
        functools.partial(_positions_body, n_experts=n_experts),
        out_shape=[jax.ShapeDtypeStruct((k, n), I32), jax.ShapeDtypeStruct((n_experts, LANES), I32)],
        scratch_shapes=[pltpu.VMEM((k, n), F32)],
        compiler_params=pltpu.CompilerParams(vmem_limit_bytes=V7X_VMEM_REQUEST),
        name="positions",
    )(idx)


def _row_copy(src_ref, src_row, dst_ref, dst_row, sem):
    return pltpu.make_async_copy(src_ref.at[pl.ds(src_row, 1), :], dst_ref.at[pl.ds(dst_row, 1), :], sem)


def _dispatch_body(pos_ref, hf_ref, xs_ref, sem):
    tm = hf_ref.shape[0]

    def issue(t, carry):
        for k in range(TOP_K):
            _row_copy(hf_ref, t, xs_ref, pos_ref[k, t], sem).start()
        return carry

    lax.fori_loop(0, tm, issue, 0)

    def drain(t, carry):
        for k in range(TOP_K):
            _row_copy(hf_ref, t, xs_ref, pos_ref[k, t], sem).wait()
        return carry

    lax.fori_loop(0, tm, drain, 0)


def _dispatch(pos, hf):
    n, d = hf.shape
    tm = TOKEN_TILE
    return pl.pallas_call(
        _dispatch_body,
        grid=(n // tm,),
        in_specs=[pl.BlockSpec((TOP_K, tm), lambda i: (0, i), memory_space=pltpu.SMEM),
                  pl.BlockSpec((tm, d), lambda i: (i, 0))],
        out_specs=pl.BlockSpec(memory_space=pl.ANY),
        out_shape=jax.ShapeDtypeStruct((TOP_K * n, d), F32),
        scratch_shapes=[pltpu.SemaphoreType.DMA(())],
        compiler_params=_params(1),
        name="dispatch",
    )(pos, hf)


def _gmm_body(tile_ref, exp_ref, lo_ref, hi_ref, first_ref, nitems_ref,
              xs_ref, w1_ref, b1_ref, w2_ref, b2_ref, ys_ref, w1b_ref, w2b_ref):
    w = pl.program_id(0)

    @pl.when(w < nitems_ref[0])
    def _():
        new_expert = jnp.logical_or(w == 0, exp_ref[w] != exp_ref[jnp.maximum(w - 1, 0)])

        @pl.when(new_expert)
        def _():
            w1b_ref[...] = w1_ref[0].astype(BF16)
            w2b_ref[...] = w2_ref[0].astype(BF16)

        d_ff = w2b_ref.shape[0]
        hid = jnp.dot(xs_ref[...].astype(BF16), w1b_ref[...], preferred_element_type=F32) + b1_ref[0]
        glu = jnp.minimum(hid[:, :d_ff], SWIGLU_LIMIT)
        lin = jnp.clip(hid[:, d_ff:], -SWIGLU_LIMIT, SWIGLU_LIMIT)
        act = glu * jax.nn.sigmoid(SWIGLU_ALPHA * glu) * (lin + 1.0)
        y = jnp.dot(act.astype(BF16), w2b_ref[...], preferred_element_type=F32) + b2_ref[0]
        rows = lax.broadcasted_iota(I32, (ys_ref.shape[0], 1), 0)
        y = jnp.where((rows >= lo_ref[w]) & (rows < hi_ref[w]), y, 0.0)

        @pl.when(first_ref[w] == 1)
        def _():
            ys_ref[...] = y

        @pl.when(first_ref[w] == 0)
        def _():
            ys_ref[...] += y


def _group_work_items(counts, n_rows, n_items):
    tr = MOE_ROW_TILE
    ends = jnp.cumsum(counts)
    starts = ends - counts
    first_tile = starts // tr
    last_tile = jnp.maximum(ends - 1, 0) // tr
    per_expert = jnp.where(counts > 0, last_tile - first_tile + 1, 0)
    item_end = jnp.cumsum(per_expert)
    total = item_end[-1]
    w = jnp.minimum(jnp.arange(n_items, dtype=I32), total - 1)
    e = jnp.minimum(jnp.searchsorted(item_end, w, side="right").astype(I32), counts.shape[0] - 1)
    tile = first_tile[e] + (w - (item_end[e] - per_expert[e]))
    valid = jnp.arange(n_items, dtype=I32) < total
    lo = jnp.where(valid, jnp.maximum(starts[e] - tile * tr, 0), 0)
    hi = jnp.where(valid, jnp.minimum(ends[e] - tile * tr, tr), 0)
    first = jnp.concatenate([jnp.ones((1,), I32), (tile[1:] != tile[:-1]).astype(I32)])
    as_i32 = lambda a: a.astype(I32)
    return as_i32(tile), as_i32(e), as_i32(lo), as_i32(hi), as_i32(first), as_i32(total).reshape(1)


def _gmm(xs, counts, w1, b1, w2, b2):
    n_rows, d = xs.shape
    n_exp, _, two_ff = w1.shape
    d_ff = two_ff // 2
    tr = MOE_ROW_TILE
    n_items = n_rows // tr + n_exp - 1
    items = _group_work_items(counts, n_rows, n_items)
    grid_spec = pltpu.PrefetchScalarGridSpec(
        num_scalar_prefetch=6,
        grid=(n_items,),
        in_specs=[
            pl.BlockSpec((tr, d), lambda w, tile, *_: (tile[w], 0)),
            pl.BlockSpec((1, d, two_ff), lambda w, tile, exp, *_: (exp[w], 0, 0)),
            pl.BlockSpec((1, 1, two_ff), lambda w, tile, exp, *_: (exp[w], 0, 0)),
            pl.BlockSpec((1, d_ff, d), lambda w, tile, exp, *_: (exp[w], 0, 0)),
            pl.BlockSpec((1, 1, d), lambda w, tile, exp, *_: (exp[w], 0, 0)),
        ],
        out_specs=pl.BlockSpec((tr, d), lambda w, tile, *_: (tile[w], 0)),
        scratch_shapes=[pltpu.VMEM((d, two_ff), BF16), pltpu.VMEM((d_ff, d), BF16)],
    )
    return pl.pallas_call(
        _gmm_body,
        grid_spec=grid_spec,
        out_shape=jax.ShapeDtypeStruct((n_rows, d), F32),
        compiler_params=_params(1),
        name="moe_gmm",
    )(*items, xs, w1, b1.reshape(n_exp, 1, two_ff), w2, b2.reshape(n_exp, 1, d))


def _combine_body(pos_ref, x2_ref, gate_ref, gain_ref, ys_ref, yp_ref, ysm_ref, buf_ref, sem, *, n_prompt_tiles):
    tm = x2_ref.shape[0]

    def issue(t, carry):
        for k in range(TOP_K):
            _row_copy(ys_ref, pos_ref[k, t], buf_ref.at[k], t, sem).start()
        return carry

    lax.fori_loop(0, tm, issue, 0)

    def drain(t, carry):
        for k in range(TOP_K):
            _row_copy(ys_ref, pos_ref[k, t], buf_ref.at[k], t, sem).wait()
        return carry

    lax.fori_loop(0, tm, drain, 0)
    gate = gate_ref[...]
    x3 = x2_ref[...]
    for k in range(TOP_K):
        x3 = x3 + gate[:, k:k + 1] * buf_ref[k]
    y = _rms(x3, gain_ref[...])
    is_prompt = pl.program_id(0) < n_prompt_tiles

    @pl.when(is_prompt)
    def _():
        yp_ref[...] = y

    @pl.when(jnp.logical_not(is_prompt))
    def _():
        ysm_ref[...] = y


def _combine(pos, x2, gate_t, gain, ys, n_prompt_rows):
    n, d = x2.shape
    tm = TOKEN_TILE
    npt = n_prompt_rows // tm
    row = lambda i: (i, 0)
    return pl.pallas_call(
        functools.partial(_combine_body, n_prompt_tiles=npt),
        grid=(n // tm,),
        in_specs=[pl.BlockSpec((TOP_K, tm), lambda i: (0, i), memory_space=pltpu.SMEM),
                  pl.BlockSpec((tm, d), row), pl.BlockSpec((tm, LANES), row), _const_spec(gain.shape),
                  pl.BlockSpec(memory_space=pl.ANY)],
        out_specs=[pl.BlockSpec((tm, d), lambda i: (jnp.minimum(i, npt - 1), 0)),
                   pl.BlockSpec((tm, d), lambda i: (jnp.maximum(i - npt, 0), 0))],
        out_shape=[jax.ShapeDtypeStruct((n_prompt_rows, d), F32),
                   jax.ShapeDtypeStruct((n - n_prompt_rows, d), F32)],
        scratch_shapes=[pltpu.VMEM((TOP_K, tm, d), F32), pltpu.SemaphoreType.DMA(())],
        compiler_params=_params(1),
        name="moe_combine",
    )(pos, x2, gate_t, gain, ys)


def kernel(x_prompt, x_sample, mem_prompt, state_gdn, state_gdn_conv, state_conf_conv, cache_mem_k, cache_mem_v,
           norm_mix, w_in, gdn_conv_w, gdn_a_log, gdn_dt_bias, gdn_norm, gdn_o, conf_conv_w, conf_conv_b,
           conf_ln_g, conf_ln_b, conf_pw2, conf_pw2_b, w_out, norm_xa, norm_mem, xa_q, xa_kv, xa_o,
           norm_ffn, router_w, router_b, moe_w1, moe_b1, moe_w2, moe_b2, norm_final):
    n_batch, seq, d = x_prompt.shape
    n_seq, n_tok, _ = x_sample.shape
    depth, _, n_heads, dk, dv = state_gdn.shape
    assert depth == 1 and dk == LANES and dv == LANES
    mem_len, xa_heads = cache_mem_k.shape[2], cache_mem_k.shape[3]
    n_experts = router_w.shape[2]
    qkv_w = gdn_conv_w.shape[2]
    v_w = n_heads * dv
    n_p, n_s = n_batch * seq, n_seq * n_tok
    assert seq % TOKEN_TILE == 0 and n_s % TOKEN_TILE == 0 and n_seq % SEQ_BLOCK == 0
    assert n_tok * SEQ_BLOCK <= LANES and 2 * n_heads <= LANES and n_tok >= gdn_conv_w.shape[1] - 1
    l = 0
    row = lambda a: a.reshape(1, -1)
    lanes = lambda a: jnp.broadcast_to(a.reshape(-1, 1), (a.shape[0], LANES))

    xp = x_prompt.reshape(n_p, d)
    xs = x_sample.transpose(1, 0, 2).reshape(n_s, d)

    w = w_in[l]
    ab_lo, ab_hi = qkv_w + v_w, qkv_w + v_w + 2 * n_heads
    wqkvz = w[:, :ab_lo].astype(BF16)
    wab = jnp.pad(w[:, ab_lo:ab_hi], ((0, 0), (0, LANES - 2 * n_heads))).astype(BF16)
    wglu = w[:, ab_hi:ab_hi + 2 * d].astype(BF16)
    wgate = w[:, ab_hi + 2 * d:].astype(BF16)
    qkv, z, ab, cv, sg = _inproj(xp, xs, row(norm_mix[l]), wqkvz, wab, wglu, wgate, qkv_w)

    alog, dtb, gnorm = lanes(gdn_a_log[l]), lanes(gdn_dt_bias[l]), row(gdn_norm[l])
    og_p, s_p = _gdn_prompt(qkv, z, ab, gdn_conv_w[l], alog, dtb, gnorm, n_batch, seq, n_heads, dk)
    og_s, s_s = _gdn_sample(qkv, z, ab, state_gdn_conv[l], state_gdn[l], gdn_conv_w[l], alog, dtb, gnorm,
                            n_p, n_seq, n_tok, n_heads, dk)

    conf_args = (conf_conv_w[l], row(conf_conv_b[l]), row(conf_ln_g[l]), row(conf_ln_b[l]))
    yb_p = _conf_prompt(cv, *conf_args, n_batch, seq)
    yb_s = _conf_sample(cv, state_conf_conv[l], *conf_args, n_p, n_seq, n_tok)

    x1, q = _merge(og_p, og_s, yb_p, yb_s, sg, xp, xs, gdn_o[l].astype(BF16), conf_pw2[l].astype(BF16),
                   row(conf_pw2_b[l]), w_out[l].astype(BF16), row(norm_xa[l]), xa_q[l].astype(BF16))

    mk_p, mv_p = _memkv(mem_prompt.reshape(n_batch * mem_len, d), row(norm_mem[l]), xa_kv[l].astype(BF16))
    o_p = _attn_prompt(q, mk_p, mv_p, n_batch, seq, mem_len, xa_heads)
    o_s = _attn_sample(q, cache_mem_k[l].reshape(n_seq, mem_len, d), cache_mem_v[l].reshape(n_seq, mem_len, d),
                       n_p, n_seq, n_tok, xa_heads)

    x2, hf, idx, gate_t = _router(o_p, o_s, x1, xa_o[l].astype(BF16), row(norm_ffn[l]),
                                  router_w[l].T, router_b[l].reshape(-1, 1))
    pos, counts = _positions(idx, n_experts)
    xsorted = _dispatch(pos, hf)
    ysorted = _gmm(xsorted, counts[:, 0], moe_w1[l], moe_b1[l], moe_w2[l], moe_b2[l])
    y_p, y_s = _combine(pos, x2, gate_t, row(norm_final), ysorted, n_p)

    y_prompt = y_p.reshape(n_batch, seq, d)
    y_sample = y_s.reshape(n_tok, n_seq, d).transpose(1, 0, 2)
    n_gc = gdn_conv_w.shape[1] - 1
    n_cc = conf_conv_w.shape[1] - 1
    qkv_p = qkv[:n_p].reshape(n_batch, seq, qkv_w)
    qkv_s = qkv[n_p:].reshape(n_tok, n_seq, qkv_w).transpose(1, 0, 2)
    cv_p = cv[:n_p].reshape(n_batch, seq, d)
    cv_s = cv[n_p:].reshape(n_tok, n_seq, d).transpose(1, 0, 2)
    gconv_p = qkv_p[:, seq - n_gc:, :].astype(F32)
    gconv_s = qkv_s[:, n_tok - n_gc:, :].astype(F32)
    cconv_p = cv_p[:, seq - n_cc:, :]
    cconv_s = jnp.concatenate([state_conf_conv[l], cv_s], axis=1)[:, n_tok:, :]
    kv_shape = (1, n_batch, mem_len, xa_heads, d // xa_heads)
    return (y_prompt, y_sample, s_p[None], gconv_p[None], cconv_p[None], mk_p.reshape(kv_shape),
            mv_p.reshape(kv_shape), s_s[None], gconv_s[None], cconv_s[None])
```

```python
import functools

import jax
import jax.numpy as jnp
from jax import lax
from jax.experimental import pallas as pl
from jax.experimental.pallas import tpu as pltpu

F32, BF16, I32 = jnp.float32, jnp.bfloat16, jnp.int32
HIGHEST = lax.Precision.HIGHEST

LANES = 128
SUBLANES = 8
V7X_VMEM_REQUEST = 56 * 1024 * 1024

TOKEN_TILE = 512
GDN_CHUNK = 64
GDN_GROUP = 256
GDN_HEADS_PER_STEP = 4
SEQ_BLOCK = 16
ATTN_SEQ_BLOCK = 4
CONF_HALO = 32
CONF_ROWS = 16
MOE_ROW_TILE = 256
CUMSUM_CHUNK = 512

TOP_K = 4
SWIGLU_ALPHA = 1.702
SWIGLU_LIMIT = 7.0
RMS_EPS = 1e-6
LN_EPS = 1e-5
L2_EPS = 1e-6


def _params(n_axes):
    return pltpu.CompilerParams(dimension_semantics=("arbitrary",) * n_axes, vmem_limit_bytes=V7X_VMEM_REQUEST)


def _const_spec(shape):
    zeros = (0,) * len(shape)
    return pl.BlockSpec(shape, lambda *_: zeros, pipeline_mode=pl.Buffered(1))


def _layer_spec(shape):
    zeros = (0,) * len(shape)
    return pl.BlockSpec((None,) + tuple(shape[1:]), lambda *_: zeros, pipeline_mode=pl.Buffered(1))


def _mm(a, b):
    return jnp.dot(a.astype(BF16), b.astype(BF16), preferred_element_type=F32)


def _mm_nt(a, b):
    return lax.dot_general(a.astype(BF16), b.astype(BF16), (((1,), (1,)), ((), ())), preferred_element_type=F32)


def _rms(x, gain):
    return x * lax.rsqrt(jnp.mean(x * x, axis=-1, keepdims=True) + RMS_EPS) * gain


def _silu(x):
    return x * jax.nn.sigmoid(x)


def _softplus(x):
    return jnp.maximum(x, 0.0) + jnp.log1p(jnp.exp(-jnp.abs(x)))


def _split_hi_lo(x):
    hi = x.astype(BF16)
    lo = (x - hi.astype(F32)).astype(BF16)
    return hi, lo


def _inproj_body(xp_ref, xs_ref, gain_ref, wqkvz_ref, wab_ref, wglu_ref, wgate_ref,
                 qkv_ref, z_ref, ab_ref, cv_ref, sg_ref, *, n_prompt_tiles):
    i = pl.program_id(0)
    d = xp_ref.shape[1]
    x = jnp.where(i < n_prompt_tiles, xp_ref[...], xs_ref[...])
    h = _rms(x, gain_ref[...]).astype(BF16)
    qkv_w = qkv_ref.shape[1]
    for c in range(0, qkv_w, d):
        qkv_ref[:, c:c + d] = jnp.dot(h, wqkvz_ref[:, c:c + d], preferred_element_type=F32).astype(BF16)
    z_ref[...] = jnp.dot(h, wqkvz_ref[:, qkv_w:], preferred_element_type=F32).astype(BF16)
    ab_ref[...] = jnp.dot(h, wab_ref[...], preferred_element_type=F32)
    glu_a = jnp.dot(h, wglu_ref[:, :d], preferred_element_type=F32)
    glu_b = jnp.dot(h, wglu_ref[:, d:], preferred_element_type=F32)
    cv_ref[...] = glu_a * jax.nn.sigmoid(glu_b)
    for c in range(0, 2 * d, d):
        sg_ref[:, c:c + d] = jax.nn.sigmoid(
            jnp.dot(h, wgate_ref[:, c:c + d], preferred_element_type=F32)).astype(BF16)


def _inproj(xp, xs, gain, wqkvz, wab, wglu, wgate, qkv_w):
    n_p, d = xp.shape
    n_s = xs.shape[0]
    tm = TOKEN_TILE
    npt, nst = n_p // tm, n_s // tm
    n = n_p + n_s
    ab_w = wab.shape[1]
    row = lambda i: (i, 0)
    return pl.pallas_call(
        functools.partial(_inproj_body, n_prompt_tiles=npt),
        grid=(npt + nst,),
        in_specs=[
            pl.BlockSpec((tm, d), lambda i: (jnp.minimum(i, npt - 1), 0)),
            pl.BlockSpec((tm, d), lambda i: (jnp.maximum(i - npt, 0), 0)),
            _const_spec(gain.shape), _const_spec(wqkvz.shape), _const_spec(wab.shape),
            _const_spec(wglu.shape), _const_spec(wgate.shape),
        ],
        out_specs=[
            pl.BlockSpec((tm, qkv_w), row), pl.BlockSpec((tm, d), row), pl.BlockSpec((tm, ab_w), row),
            pl.BlockSpec((tm, d), row), pl.BlockSpec((tm, 2 * d), row),
        ],
        out_shape=[
            jax.ShapeDtypeStruct((n, qkv_w), BF16), jax.ShapeDtypeStruct((n, d), BF16),
            jax.ShapeDtypeStruct((n, ab_w), F32), jax.ShapeDtypeStruct((n, d), F32),
            jax.ShapeDtypeStruct((n, 2 * d), BF16),
        ],
        compiler_params=_params(1),
        name="inproj",
    )(xp, xs, gain, wqkvz, wab, wglu, wgate)


def _decay_scalars(ab, alog_row, dtb_row):
    return -jnp.exp(alog_row) * _softplus(ab + dtb_row), jax.nn.sigmoid(ab)


def _cumsum_rows(incl_bf16, g):
    hi, lo = _split_hi_lo(g)
    both = jnp.dot(incl_bf16, jnp.concatenate([hi, lo], axis=1), preferred_element_type=F32)
    return both[:, :LANES] + both[:, LANES:]


def _l2norm(x):
    return x * lax.rsqrt(jnp.sum(x * x, axis=-1, keepdims=True) + L2_EPS)


def _pair_masks(r, same):
    ri = lax.broadcasted_iota(I32, (r, r), 0)
    ci = lax.broadcasted_iota(I32, (r, r), 1)
    pair = same(ri, ci)
    as_f = lambda m: jnp.where(m, 1.0, 0.0).astype(F32)
    return as_f(ri == ci), as_f(pair & (ri >= ci)), -as_f(pair & (ri > ci))


def _lane_col(x, lane):
    return jnp.broadcast_to(x[:, lane:lane + 1], (x.shape[0], LANES))


def _wy_heads(q, k, v, g_all, beta_all, hps, eye, incl, neg_strict, n_doublings):
    r = q[0].shape[0]
    heads = range(len(q))
    gc_all = _cumsum_rows(incl.astype(BF16), g_all)
    pad = jnp.zeros((max(LANES - r, 0), LANES), F32)
    gc_t = jnp.transpose(jnp.concatenate([gc_all, pad], axis=0) if r < LANES else gc_all)
    gc = [_lane_col(gc_all, h) for h in heads]
    beta = [_lane_col(beta_all, hps + h) for h in heads]
    pair_decay = []
    for h in heads:
        gc_wide = jnp.concatenate([gc[h]] * (r // LANES), axis=1) if r > LANES else gc[h][:, :r]
        pair_decay.append(jnp.exp(jnp.minimum(gc_wide - gc_t[h:h + 1, :r], 0.0)))
    kb = [k[h] * beta[h] for h in heads]
    gram = [_mm_nt(jnp.concatenate([kb[h], q[h]], axis=0), k[h]) for h in heads]
    neg_m = [gram[h][:r] * (pair_decay[h] * neg_strict) for h in heads]
    p = [gram[h][r:] * (pair_decay[h] * incl) for h in heads]
    t_inv = [eye + neg_m[h] for h in heads]
    power = neg_m
    for _ in range(n_doublings):
        power = [_mm(power[h], power[h]) for h in heads]
        t_inv = [t_inv[h] + _mm(t_inv[h], power[h]) for h in heads]
    sol = [_mm(t_inv[h], jnp.concatenate([v[h] * beta[h], kb[h] * jnp.exp(gc[h])], axis=1)) for h in heads]
    return gc, [s[:, :LANES] for s in sol], [s[:, LANES:] for s in sol], p


def _gated_out(o, gnorm, z):
    on = o * lax.rsqrt(jnp.mean(o * o, axis=-1, keepdims=True) + RMS_EPS) * gnorm
    return (on * _silu(z)).astype(BF16)


def _gdn_prompt_body(q_ref, k_ref, v_ref, z_ref, ab_ref, cwq_ref, cwk_ref, cwv_ref, alog_ref, dtb_ref,
                     gnorm_ref, og_ref, sout_ref, tq_ref, tk_ref, tv_ref, halo_ref, s_ref, mask_ref, *, n_taps):
    hps, dk = s_ref.shape[0], s_ref.shape[1]
    grp, chunk = GDN_GROUP, GDN_CHUNK
    n_groups = q_ref.shape[0] // grp
    halo_rows = halo_ref.shape[1]
    halo_ref[...] = jnp.zeros(halo_ref.shape, F32)
    s_ref[...] = jnp.zeros(s_ref.shape, F32)
    for i, m in enumerate(_pair_masks(grp, lambda ri, ci: (ri // chunk) == (ci // chunk))):
        mask_ref[i] = m
    heads = range(hps)

    def group(gi, carry):
        rows = pl.ds(pl.multiple_of(gi * grp, grp), grp)
        conv = []
        for part, (x_ref, cw_ref) in enumerate(((q_ref, cwq_ref), (k_ref, cwk_ref), (v_ref, cwv_ref))):
            xg = x_ref[rows, :].astype(F32)
            xw = jnp.concatenate([halo_ref[part], xg], axis=0)
            halo_ref[part] = xg[grp - halo_rows:, :]
            y = None
            for j in range(n_taps):
                off = halo_rows - (n_taps - 1) + j
                term = cw_ref[j:j + 1, :] * xw[off:off + grp, :]
                y = term if y is None else y + term
            conv.append(_silu(y))
        sl = [slice(h * dk, (h + 1) * dk) for h in heads]
        q = [_l2norm(conv[0][:, sl[h]]) * (dk ** -0.5) for h in heads]
        k = [_l2norm(conv[1][:, sl[h]]) for h in heads]
        v = [conv[2][:, sl[h]] for h in heads]
        g_all, beta_all = _decay_scalars(ab_ref[rows, :], alog_ref[0:1, :], dtb_ref[0:1, :])
        gc, u, w, p = _wy_heads(q, k, v, g_all, beta_all, hps, mask_ref[0], mask_ref[1], mask_ref[2], 5)
        qe = [q[h] * jnp.exp(gc[h]) for h in heads]
        state = [s_ref[h] for h in heads]
        outs = [[] for _ in heads]
        for j in range(grp // chunk):
            cs = slice(j * chunk, (j + 1) * chunk)
            g_last = [gc[h][(j + 1) * chunk - 1:(j + 1) * chunk, :] for h in heads]
            k_dec = [k[h][cs] * jnp.exp(g_last[h] - gc[h][cs]) for h in heads]
            ws_qs = [_mm(jnp.concatenate([w[h][cs], qe[h][cs]], axis=0), state[h]) for h in heads]
            v_new = [u[h][cs] - ws_qs[h][:chunk] for h in heads]
            for h in heads:
                outs[h].append(ws_qs[h][chunk:] + _mm(p[h][cs, cs], v_new[h]))
            state = [state[h] * jnp.exp(g_last[h]) + lax.dot_general(
                k_dec[h].astype(BF16), v_new[h].astype(BF16), (((0,), (0,)), ((), ())),
                preferred_element_type=F32) for h in heads]
        for h in heads:
            s_ref[h] = state[h]
            o = jnp.concatenate(outs[h], axis=0)
            og_ref[rows, sl[h]] = _gated_out(o, gnorm_ref[...], z_ref[rows, sl[h]].astype(F32))
        return carry

    lax.fori_loop(0, n_groups, group, 0)
    sout_ref[0] = s_ref[...]
    tq_ref[0] = halo_ref[0]
    tk_ref[0] = halo_ref[1]
    tv_ref[0] = halo_ref[2]


def _gdn_prompt(qkv, z, ab, conv_w, alog, dtb, gnorm, n_batch, seq, n_heads, dk):
    hps = GDN_HEADS_PER_STEP
    hgw = hps * dk
    n_hg = n_heads // hps
    n_taps = conv_w.shape[1]
    grp = GDN_GROUP
    col = lambda part: (lambda b, h: (b, part * n_hg + h))
    cwcol = lambda part: (lambda b, h: (0, 0, part * n_hg + h))
    bh = lambda b, h: (b, h)
    tail = jax.ShapeDtypeStruct((n_batch, SUBLANES, n_heads * dk), F32)
    tail_spec = pl.BlockSpec((1, SUBLANES, hgw), lambda b, h: (b, 0, h))
    return pl.pallas_call(
        functools.partial(_gdn_prompt_body, n_taps=n_taps),
        grid=(n_batch, n_hg),
        in_specs=[
            pl.BlockSpec((seq, hgw), col(0)), pl.BlockSpec((seq, hgw), col(1)), pl.BlockSpec((seq, hgw), col(2)),
            pl.BlockSpec((seq, hgw), bh),
            pl.BlockSpec((seq, LANES), bh),
            pl.BlockSpec((None, n_taps, hgw), cwcol(0)), pl.BlockSpec((None, n_taps, hgw), cwcol(1)),
            pl.BlockSpec((None, n_taps, hgw), cwcol(2)),
            pl.BlockSpec((SUBLANES, LANES), lambda b, h: (h, 0)),
            pl.BlockSpec((SUBLANES, LANES), lambda b, h: (h, 0)),
            _const_spec(gnorm.shape),
        ],
        out_specs=[
            pl.BlockSpec((seq, hgw), bh),
            pl.BlockSpec((1, hps, dk, dk), lambda b, h: (b, h, 0, 0)),
            tail_spec, tail_spec, tail_spec,
        ],
        out_shape=[
            jax.ShapeDtypeStruct((n_batch * seq, n_heads * dk), BF16),
            jax.ShapeDtypeStruct((n_batch, n_heads, dk, dk), F32),
            tail, tail, tail,
        ],
        scratch_shapes=[pltpu.VMEM((3, SUBLANES, hgw), F32), pltpu.VMEM((hps, dk, dk), F32),
                        pltpu.VMEM((3, grp, grp), F32)],
        compiler_params=_params(2),
        name="gdn_prompt",
    )(qkv, qkv, qkv, z, ab, conv_w, conv_w, conv_w, alog, dtb, gnorm)


def _gdn_sample_body(*refs, n_tok, n_taps):
    sb = SEQ_BLOCK
    it = iter(refs)
    take = lambda n: [next(it) for _ in range(n)]
    x_refs = [take(n_tok) for _ in range(3)]
    cb_refs = take(3)
    z_refs = take(n_tok)
    ab_refs = take(n_tok)
    s0_ref = next(it)
    cw_refs = take(3)
    alog_ref, dtb_ref, gnorm_ref = take(3)
    og_refs = take(n_tok)
    sout_ref = next(it)

    hps, dk = s0_ref.shape[1], s0_ref.shape[2]
    heads = range(hps)
    r = n_tok * sb

    conv = []
    for part in range(3):
        ext = [cb_refs[part][:, j, :] for j in range(n_taps - 1)]
        ext += [x_refs[part][t][...].astype(F32) for t in range(n_tok)]
        ys = []
        for t in range(n_tok):
            y = None
            for j in range(n_taps):
                term = cw_refs[part][j:j + 1, :] * ext[t + j]
                y = term if y is None else y + term
            ys.append(_silu(y))
        conv.append(jnp.concatenate(ys, axis=0))
    ab = jnp.concatenate([a[...] for a in ab_refs], axis=0)

    eye, incl, neg_strict = _pair_masks(r, lambda ri, ci: (ri % sb) == (ci % sb))
    row_seq = lax.broadcasted_iota(I32, (2 * r, dk), 0) % sb
    col_seq = lax.broadcasted_iota(I32, (dk, LANES), 1) % sb
    n_doublings = max((n_tok - 1).bit_length() - 1, 0)

    sl = [slice(h * dk, (h + 1) * dk) for h in heads]
    q = [_l2norm(conv[0][:, sl[h]]) * (dk ** -0.5) for h in heads]
    k = [_l2norm(conv[1][:, sl[h]]) for h in heads]
    v = [conv[2][:, sl[h]] for h in heads]
    g_all, beta_all = _decay_scalars(ab, alog_ref[0:1, :], dtb_ref[0:1, :])
    gc, u, w, p = _wy_heads(q, k, v, g_all, beta_all, hps, eye, incl, neg_strict, n_doublings)
    for h in heads:
        wq = jnp.concatenate([w[h], q[h] * jnp.exp(gc[h])], axis=0).astype(BF16)
        ws_qs = jnp.zeros((2 * r, dk), F32)
        for s in range(sb):
            ws_qs = jnp.where(row_seq == s, _mm(wq, s0_ref[s, h]), ws_qs)
        v_new = u[h] - ws_qs[:r]
        o = ws_qs[r:] + _mm(p[h], v_new)
        z = jnp.concatenate([zr[:, sl[h]] for zr in z_refs], axis=0).astype(F32)
        og = _gated_out(o, gnorm_ref[...], z)
        for t in range(n_tok):
            og_refs[t][:, sl[h]] = og[t * sb:(t + 1) * sb]
        g_end = gc[h][r - sb:, :]
        k_dec = k[h] * jnp.exp(jnp.concatenate([g_end] * n_tok, axis=0) - gc[h])
        k_dec_t = jnp.transpose(jnp.concatenate([k_dec, jnp.zeros((LANES - r, dk), F32)], axis=0))
        v_new_pad = jnp.concatenate([v_new, jnp.zeros((LANES - r, dk), F32)], axis=0).astype(BF16)
        for s in range(sb):
            upd = _mm(jnp.where(col_seq == s, k_dec_t, 0.0), v_new_pad)
            sout_ref[s, h] = s0_ref[s, h] * jnp.exp(g_end[s:s + 1, :]) + upd


def _gdn_sample(qkv, z, ab, conv_buf, s0, conv_w, alog, dtb, gnorm, n_prompt_rows, n_seq, n_tok, n_heads, dk):
    sb, hps = SEQ_BLOCK, GDN_HEADS_PER_STEP
    hgw = hps * dk
    n_hg = n_heads // hps
    n_taps = conv_w.shape[1]
    base = n_prompt_rows // sb
    per_t = n_seq // sb

    def tok_rows(t, colfn):
        return lambda i, h: (base + t * per_t + i, colfn(h))

    in_specs, args = [], []
    for part in range(3):
        for t in range(n_tok):
            in_specs.append(pl.BlockSpec((sb, hgw), tok_rows(t, lambda h, part=part: part * n_hg + h)))
            args.append(qkv)
    for part in range(3):
        in_specs.append(pl.BlockSpec((None, sb, n_taps - 1, hgw),
                                     lambda i, h, part=part: (0, i, 0, part * n_hg + h)))
        args.append(conv_buf)
    for t in range(n_tok):
        in_specs.append(pl.BlockSpec((sb, hgw), tok_rows(t, lambda h: h)))
        args.append(z)
    for t in range(n_tok):
        in_specs.append(pl.BlockSpec((sb, LANES), tok_rows(t, lambda h: h)))
        args.append(ab)
    in_specs.append(pl.BlockSpec((None, sb, hps, dk, dk), lambda i, h: (0, i, h, 0, 0)))
    args.append(s0)
    for part in range(3):
        in_specs.append(pl.BlockSpec((None, n_taps, hgw), lambda i, h, part=part: (0, 0, part * n_hg + h)))
        args.append(conv_w)
    in_specs += [pl.BlockSpec((SUBLANES, LANES), lambda i, h: (h, 0)),
                 pl.BlockSpec((SUBLANES, LANES), lambda i, h: (h, 0)), _const_spec(gnorm.shape)]
    args += [alog, dtb, gnorm]
    outs = pl.pallas_call(
        functools.partial(_gdn_sample_body, n_tok=n_tok, n_taps=n_taps),
        grid=(per_t, n_hg),
        in_specs=in_specs,
        out_specs=[pl.BlockSpec((sb, hgw), lambda i, h: (i, h))] * n_tok
        + [pl.BlockSpec((sb, hps, dk, dk), lambda i, h: (i, h, 0, 0))],
        out_shape=[jax.ShapeDtypeStruct((n_seq, n_heads * dk), BF16)] * n_tok
        + [jax.ShapeDtypeStruct(s0.shape[1:], F32)],
        compiler_params=_params(2),
        name="gdn_sample",
    )(*args)
    return jnp.concatenate(outs[:n_tok], axis=0), outs[n_tok]


def _ln_silu(y, lg, lb):
    mu = jnp.mean(y, axis=-1, keepdims=True)
    yc = y - mu
    var = jnp.mean(yc * yc, axis=-1, keepdims=True)
    return _silu(yc * lax.rsqrt(var + LN_EPS) * lg + lb).astype(BF16)


def _conf_prompt_body(cv_ref, halo_ref, cw_ref, cb_ref, lg_ref, lb_ref, out_ref, tail_ref, rot_ref, *, n_taps):
    tile_in_seq = pl.program_id(1)
    tt, c = cv_ref.shape
    halo = CONF_HALO
    rot_ref[0, 0:halo, :] = jnp.where(tile_in_seq == 0, 0.0, halo_ref[...])
    rot_ref[0, halo:halo + tt, :] = cv_ref[...]
    rot_ref[0, halo + tt:halo + tt + SUBLANES, :] = jnp.zeros((SUBLANES, c), F32)
    step = 32

    def rotate(i, carry):
        base = pl.multiple_of(i * step, step)
        win = rot_ref[0, pl.ds(base, step + SUBLANES), :]
        for r in range(1, SUBLANES):
            rot_ref[r, pl.ds(base, step), :] = win[r:r + step, :]
        return carry

    lax.fori_loop(0, (halo + tt) // step, rotate, 0)
    lead = halo - (n_taps - 1)
    rows = CONF_ROWS

    def conv(i, carry):
        base = pl.multiple_of(i * rows, rows)
        acc = jnp.broadcast_to(cb_ref[...], (rows, c))
        for j in range(n_taps):
            off = lead + j
            start = pl.multiple_of(base + (off // SUBLANES) * SUBLANES, SUBLANES)
            acc = acc + cw_ref[j:j + 1, :] * rot_ref[off % SUBLANES, pl.ds(start, rows), :]
        out_ref[pl.ds(base, rows), :] = _ln_silu(acc, lg_ref[...], lb_ref[...])
        return carry

    lax.fori_loop(0, tt // rows, conv, 0)

    @pl.when(tile_in_seq == pl.num_programs(1) - 1)
    def _():
        tail_ref[0] = cv_ref[tt - halo:, :]


def _conf_prompt(cv, cw, cb, lg, lb, n_batch, seq):
    c = cv.shape[1]
    tt = TOKEN_TILE
    tiles = seq // tt
    halo_per_tile = tt // CONF_HALO
    n_taps = cw.shape[1]
    return pl.pallas_call(
        functools.partial(_conf_prompt_body, n_taps=n_taps),
        grid=(n_batch, tiles),
        in_specs=[
            pl.BlockSpec((tt, c), lambda b, j: (b * tiles + j, 0)),
            pl.BlockSpec((CONF_HALO, c), lambda b, j: (jnp.maximum((b * tiles + j) * halo_per_tile - 1, 0), 0)),
            _layer_spec(cw.shape), _const_spec(cb.shape), _const_spec(lg.shape), _const_spec(lb.shape),
        ],
        out_specs=[pl.BlockSpec((tt, c), lambda b, j: (b * tiles + j, 0)),
                   pl.BlockSpec((1, CONF_HALO, c), lambda b, j: (b, 0, 0))],
        out_shape=[jax.ShapeDtypeStruct((n_batch * seq, c), BF16),
                   jax.ShapeDtypeStruct((n_batch, CONF_HALO, c), F32)],
        scratch_shapes=[pltpu.VMEM((SUBLANES, CONF_HALO + tt + SUBLANES, c), F32)],
        compiler_params=_params(2),
        name="conf_prompt",
    )(cv, cv, cw, cb, lg, lb)


def _conf_sample_body(*refs, n_tok, n_taps):
    buf_ref = refs[0]
    cv_refs = refs[1:1 + n_tok]
    cw_ref, cb_ref, lg_ref, lb_ref = refs[1 + n_tok:5 + n_tok]
    out_refs = refs[5 + n_tok:5 + 2 * n_tok]
    newbuf_ref = refs[5 + 2 * n_tok]
    n_hist = n_taps - 1

    def ext(i):
        return buf_ref[:, i, :] if i < n_hist else cv_refs[i - n_hist][...]

    for t in range(n_tok):
        acc = jnp.broadcast_to(cb_ref[...], cv_refs[0].shape)
        for j in range(n_taps):
            acc = acc + cw_ref[j:j + 1, :] * ext(t + j)
        out_refs[t][...] = _ln_silu(acc, lg_ref[...], lb_ref[...])
    for i in range(n_hist):
        newbuf_ref[:, i, :] = ext(i + n_tok)


def _conf_sample(cv, buf, cw, cb, lg, lb, n_prompt_rows, n_seq, n_tok):
    sb = SEQ_BLOCK
    c = cv.shape[1]
    n_taps = cw.shape[1]
    base, per_t = n_prompt_rows // sb, n_seq // sb
    in_specs = [pl.BlockSpec((None, sb, n_taps - 1, c), lambda i: (0, i, 0, 0))]
    in_specs += [pl.BlockSpec((sb, c), lambda i, t=t: (base + t * per_t + i, 0)) for t in range(n_tok)]
    in_specs += [_layer_spec(cw.shape), _const_spec(cb.shape), _const_spec(lg.shape), _const_spec(lb.shape)]
    outs = pl.pallas_call(
        functools.partial(_conf_sample_body, n_tok=n_tok, n_taps=n_taps),
        grid=(per_t,),
        in_specs=in_specs,
        out_specs=[pl.BlockSpec((sb, c), lambda i: (i, 0))] * n_tok
        + [pl.BlockSpec((sb, n_taps - 1, c), lambda i: (i, 0, 0))],
        out_shape=[jax.ShapeDtypeStruct((n_seq, c), BF16)] * n_tok
        + [jax.ShapeDtypeStruct((n_seq, n_taps - 1, c), F32)],
        compiler_params=_params(1),
        name="conf_sample",
    )(buf, *([cv] * n_tok), cw, cb, lg, lb)
    return jnp.concatenate(outs[:n_tok], axis=0), outs[n_tok]


def _merge_body(ogp_ref, ogs_ref, ybp_ref, ybs_ref, sg_ref, xp_ref, xs_ref, wgo_ref, wpw_ref, bpw_ref,
                wout_ref, gxa_ref, wq_ref, x1_ref, q_ref, *, n_prompt_tiles):
    is_prompt = pl.program_id(0) < n_prompt_tiles
    d = xp_ref.shape[1]
    og = jnp.where(is_prompt, ogp_ref[...], ogs_ref[...])
    yb_in = jnp.where(is_prompt, ybp_ref[...], ybs_ref[...])
    x = jnp.where(is_prompt, xp_ref[...], xs_ref[...])
    y_a = jnp.dot(og, wgo_ref[...], preferred_element_type=F32)
    y_b = jnp.dot(yb_in, wpw_ref[...], preferred_element_type=F32) + bpw_ref[...]
    mixed = sg_ref[:, :d].astype(F32) * y_a + sg_ref[:, d:].astype(F32) * y_b
    x1 = x + jnp.dot(mixed.astype(BF16), wout_ref[...], preferred_element_type=F32)
    x1_ref[...] = x1
    q_ref[...] = jnp.dot(_rms(x1, gxa_ref[...]).astype(BF16), wq_ref[...],
                         preferred_element_type=F32).astype(BF16)


def _dual(tm, cols, npt):
    return [pl.BlockSpec((tm, cols), lambda i: (jnp.minimum(i, npt - 1), 0)),
            pl.BlockSpec((tm, cols), lambda i: (jnp.maximum(i - npt, 0), 0))]


def _merge(og_p, og_s, yb_p, yb_s, sg, xp, xs, wgo, wpw, bpw, wout, gxa, wq):
    n_p, d = xp.shape
    tm = TOKEN_TILE
    npt, nst = n_p // tm, xs.shape[0] // tm
    n = n_p + xs.shape[0]
    row = lambda i: (i, 0)
    return pl.pallas_call(
        functools.partial(_merge_body, n_prompt_tiles=npt),
        grid=(npt + nst,),
        in_specs=_dual(tm, d, npt) + _dual(tm, d, npt) + [pl.BlockSpec((tm, 2 * d), row)] + _dual(tm, d, npt)
        + [_const_spec(a.shape) for a in (wgo, wpw, bpw, wout, gxa, wq)],
        out_specs=[pl.BlockSpec((tm, d), row), pl.BlockSpec((tm, d), row)],
        out_shape=[jax.ShapeDtypeStruct((n, d), F32), jax.ShapeDtypeStruct((n, d), BF16)],
        compiler_params=_params(1),
        name="merge",
    )(og_p, og_s, yb_p, yb_s, sg, xp, xs, wgo, wpw, bpw, wout, gxa, wq)


def _memkv_body(m_ref, gain_ref, w_ref, k_ref, v_ref):
    d = k_ref.shape[1]
    h = _rms(m_ref[...], gain_ref[...]).astype(BF16)
    k_ref[...] = jnp.dot(h, w_ref[:, :d], preferred_element_type=F32)
    v_ref[...] = jnp.dot(h, w_ref[:, d:], preferred_element_type=F32)


def _memkv(mem, gain, w):
    n, d = mem.shape
    tm = min(TOKEN_TILE, n)
    row = lambda i: (i, 0)
    return pl.pallas_call(
        _memkv_body,
        grid=(n // tm,),
        in_specs=[pl.BlockSpec((tm, d), row), _const_spec(gain.shape), _const_spec(w.shape)],
        out_specs=[pl.BlockSpec((tm, d), row)] * 2,
        out_shape=[jax.ShapeDtypeStruct((n, d), F32)] * 2,
        compiler_params=_params(1),
        name="memkv",
    )(mem, gain, w)


def _softmax_rows(s):
    e = jnp.exp(s - jnp.max(s, axis=-1, keepdims=True))
    return e / jnp.sum(e, axis=-1, keepdims=True)


def _attn_prompt_body(q_ref, k_ref, v_ref, o_ref, *, n_heads):
    dh = q_ref.shape[1] // n_heads
    for h in range(n_heads):
        sl = slice(h * dh, (h + 1) * dh)
        s = _mm_nt(q_ref[:, sl], k_ref[:, sl]) * (dh ** -0.5)
        o_ref[:, sl] = _mm(_softmax_rows(s), v_ref[:, sl]).astype(BF16)


def _attn_prompt(q, k, v, n_batch, seq, mem_len, n_heads):
    d = q.shape[1]
    tq = TOKEN_TILE
    tiles = seq // tq
    return pl.pallas_call(
        functools.partial(_attn_prompt_body, n_heads=n_heads),
        grid=(n_batch, tiles),
        in_specs=[
            pl.BlockSpec((tq, d), lambda b, j: (b * tiles + j, 0)),
            pl.BlockSpec((mem_len, d), lambda b, j: (b, 0)),
            pl.BlockSpec((mem_len, d), lambda b, j: (b, 0)),
        ],
        out_specs=pl.BlockSpec((tq, d), lambda b, j: (b * tiles + j, 0)),
        out_shape=jax.ShapeDtypeStruct((n_batch * seq, d), BF16),
        compiler_params=_params(2),
        name="attn_prompt",
    )(q, k, v)


def _attn_sample_body(q_ref, k_ref, v_ref, o_ref):
    sb, mem_len, n_heads, dh = k_ref.shape
    rows_per_seq = q_ref.shape[0] // sb
    for h in range(n_heads):
        sl = slice(h * dh, (h + 1) * dh)
        keys = k_ref[:, :, h, :].reshape(sb * mem_len, dh)
        vals = v_ref[:, :, h, :].reshape(sb * mem_len, dh)
        s = _mm_nt(q_ref[:, sl], keys) * (dh ** -0.5)
        row_seq = lax.broadcasted_iota(I32, s.shape, 0) // rows_per_seq
        col_seq = lax.broadcasted_iota(I32, s.shape, 1) // mem_len
        p = _softmax_rows(jnp.where(row_seq == col_seq, s, -jnp.inf))
        o_ref[:, sl] = _mm(p, vals).astype(BF16)


def _attn_sample(q_seq_major, kc, vc, n_tok):
    n_s, d = q_seq_major.shape
    sb = ATTN_SEQ_BLOCK
    rows = sb * n_tok
    cache_spec = pl.BlockSpec((None, sb) + kc.shape[2:], lambda i: (0, i, 0, 0, 0))
    return pl.pallas_call(
        _attn_sample_body,
        grid=(n_s // rows,),
        in_specs=[pl.BlockSpec((rows, d), lambda i: (i, 0)), cache_spec, cache_spec],
        out_specs=pl.BlockSpec((rows, d), lambda i: (i, 0)),
        out_shape=jax.ShapeDtypeStruct((n_s, d), BF16),
        compiler_params=_params(1),
        name="attn_sample",
    )(q_seq_major, kc, vc)


def _router_body(op_ref, os_ref, x1_ref, wo_ref, gain_ref, rwt_ref, rb_ref,
                 x2_ref, hf_ref, idx_ref, gate_ref, *, n_prompt_tiles):
    o = jnp.where(pl.program_id(0) < n_prompt_tiles, op_ref[...], os_ref[...])
    x2 = x1_ref[...] + jnp.dot(o, wo_ref[...], preferred_element_type=F32)
    x2_ref[...] = x2
    hf = _rms(x2, gain_ref[...])
    hf_ref[...] = hf
    logits = lax.dot_general(rwt_ref[...], hf, (((1,), (1,)), ((), ())), precision=HIGHEST,
                             preferred_element_type=F32) + rb_ref[...]
    n_exp, tm = logits.shape
    expert = lax.broadcasted_iota(I32, logits.shape, 0)
    vals, picks = [], []
    for _ in range(TOP_K):
        best = jnp.max(logits, axis=0, keepdims=True)
        pick = jnp.min(jnp.where(logits == best, expert, n_exp), axis=0, keepdims=True)
        vals.append(best)
        picks.append(pick)
        logits = jnp.where(expert == pick, -jnp.inf, logits)
    idx_ref[...] = jnp.concatenate(picks, axis=0)
    es = [jnp.exp(v - vals[0]) for v in vals]
    total = functools.reduce(lambda a, b: a + b, es)
    gates = jnp.concatenate([e / total for e in es] + [jnp.zeros((LANES - TOP_K, tm), F32)], axis=0)
    gate_ref[...] = jnp.transpose(gates)


def _router(o_p, o_s, x1, wo, gain, rwt, rb):
    n, d = x1.shape
    tm = TOKEN_TILE
    npt = o_p.shape[0] // tm
    row = lambda i: (i, 0)
    return pl.pallas_call(
        functools.partial(_router_body, n_prompt_tiles=npt),
        grid=(n // tm,),
        in_specs=_dual(tm, d, npt) + [pl.BlockSpec((tm, d), row)]
        + [_const_spec(a.shape) for a in (wo, gain, rwt, rb)],
        out_specs=[pl.BlockSpec((tm, d), row), pl.BlockSpec((tm, d), row),
                   pl.BlockSpec((TOP_K, tm), lambda i: (0, i)), pl.BlockSpec((tm, LANES), row)],
        out_shape=[jax.ShapeDtypeStruct((n, d), F32), jax.ShapeDtypeStruct((n, d), F32),
                   jax.ShapeDtypeStruct((TOP_K, n), I32), jax.ShapeDtypeStruct((n, LANES), F32)],
        compiler_params=_params(1),
        name="router",
    )(o_p, o_s, x1, wo, gain, rwt, rb)


def _positions_body(idx_ref, pos_ref, counts_ref, rank_ref, *, n_experts):
    ch = CUMSUM_CHUNK
    n = idx_ref.shape[1]
    n_chunks = n // ch
    upper = (lax.broadcasted_iota(I32, (ch, ch), 0) <= lax.broadcasted_iota(I32, (ch, ch), 1)).astype(BF16)
    expert = lax.broadcasted_iota(I32, (n_experts, ch), 0)

    def onehot(k, c):
        cols = pl.ds(pl.multiple_of(c * ch, ch), ch)
        return cols, expert == idx_ref[k:k + 1, cols]

    running = jnp.zeros((n_experts, 1), F32)
    for k in range(TOP_K):
        def count(c, run, k=k):
            cols, oh = onehot(k, c)
            cum = jnp.dot(oh.astype(BF16), upper, preferred_element_type=F32)
            rank_ref[k:k + 1, cols] = jnp.sum(jnp.where(oh, cum - 1.0 + run, 0.0), axis=0, keepdims=True)
            return run + cum[:, ch - 1:ch]
        running = lax.fori_loop(0, n_chunks, count, running)

    counts = jnp.broadcast_to(running, (n_experts, LANES))
    counts_ref[...] = counts.astype(I32)
    lower = (lax.broadcasted_iota(I32, (n_experts, n_experts), 0)
             > lax.broadcasted_iota(I32, (n_experts, n_experts), 1)).astype(F32)
    start = jnp.dot(lower, counts, precision=HIGHEST, preferred_element_type=F32)[:, 0:1]

    for k in range(TOP_K):
        def place(c, carry, k=k):
            cols, oh = onehot(k, c)
            first = jnp.sum(jnp.where(oh, start, 0.0), axis=0, keepdims=True)
            pos_ref[k:k + 1, cols] = (rank_ref[k:k + 1, cols] + first).astype(I32)
            return carry
        lax.fori_loop(0, n_chunks, place, 0)


def _positions(idx, n_experts):
    k, n = idx.shape
    return pl.pallas_call(
        functools.partial(_positions_body, n_experts=n_experts),
        out_shape=[jax.ShapeDtypeStruct((k, n), I32), jax.ShapeDtypeStruct((n_experts, LANES), I32)],
        scratch_shapes=[pltpu.VMEM((k, n), F32)],
        compiler_params=pltpu.CompilerParams(vmem_limit_bytes=V7X_VMEM_REQUEST),
        name="positions",
    )(idx)


def _row_copy(src_ref, src_row, dst_ref, dst_row, sem):
    return pltpu.make_async_copy(src_ref.at[pl.ds(src_row, 1), :], dst_ref.at[pl.ds(dst_row, 1), :], sem)


def _dispatch_body(pos_ref, hf_ref, xs_ref, sem):
    tm = hf_ref.shape[0]

    def issue(t, carry):
        for k in range(TOP_K):
            _row_copy(hf_ref, t, xs_ref, pos_ref[k, t], sem).start()
        return carry

    lax.fori_loop(0, tm, issue, 0)

    def drain(t, carry):
        for k in range(TOP_K):
            _row_copy(hf_ref, t, xs_ref, pos_ref[k, t], sem).wait()
        return carry

    lax.fori_loop(0, tm, drain, 0)


def _dispatch(pos, hf):
    n, d = hf.shape
    tm = TOKEN_TILE
    return pl.pallas_call(
        _dispatch_body,
        grid=(n // tm,),
        in_specs=[pl.BlockSpec((TOP_K, tm), lambda i: (0, i), memory_space=pltpu.SMEM),
                  pl.BlockSpec((tm, d), lambda i: (i, 0))],
        out_specs=pl.BlockSpec(memory_space=pl.ANY),
        out_shape=jax.ShapeDtypeStruct((TOP_K * n, d), F32),
        scratch_shapes=[pltpu.SemaphoreType.DMA(())],
        compiler_params=_params(1),
        name="dispatch",
    )(pos, hf)


def _gmm_body(tile_ref, exp_ref, lo_ref, hi_ref, first_ref, nitems_ref,
              xs_ref, w1_ref, b1_ref, w2_ref, b2_ref, ys_ref, w1b_ref, w2b_ref):
    w = pl.program_id(0)

    @pl.when(w < nitems_ref[0])
    def _():
        new_expert = jnp.logical_or(w == 0, exp_ref[w] != exp_ref[jnp.maximum(w - 1, 0)])

        @pl.when(new_expert)
        def _():
            w1b_ref[...] = w1_ref[0].astype(BF16)
            w2b_ref[...] = w2_ref[0].astype(BF16)

        d_ff = w2b_ref.shape[0]
        hid = jnp.dot(xs_ref[...].astype(BF16), w1b_ref[...], preferred_element_type=F32) + b1_ref[0]
        glu = jnp.minimum(hid[:, :d_ff], SWIGLU_LIMIT)
        lin = jnp.clip(hid[:, d_ff:], -SWIGLU_LIMIT, SWIGLU_LIMIT)
        act = glu * jax.nn.sigmoid(SWIGLU_ALPHA * glu) * (lin + 1.0)
        y = jnp.dot(act.astype(BF16), w2b_ref[...], preferred_element_type=F32) + b2_ref[0]
        rows = lax.broadcasted_iota(I32, (ys_ref.shape[0], 1), 0)
        y = jnp.where((rows >= lo_ref[w]) & (rows < hi_ref[w]), y, 0.0)

        @pl.when(first_ref[w] == 1)
        def _():
            ys_ref[...] = y

        @pl.when(first_ref[w] == 0)
        def _():
            ys_ref[...] += y


def _group_work_items(counts, n_items):
    tr = MOE_ROW_TILE
    ends = jnp.cumsum(counts)
    starts = ends - counts
    first_tile = starts // tr
    last_tile = jnp.maximum(ends - 1, 0) // tr
    per_expert = jnp.where(counts > 0, last_tile - first_tile + 1, 0)
    item_end = jnp.cumsum(per_expert)
    total = item_end[-1]
    w = jnp.minimum(jnp.arange(n_items, dtype=I32), total - 1)
    e = jnp.sum((item_end[None, :] <= w[:, None]).astype(I32), axis=1)
    onehot = (e[:, None] == jnp.arange(counts.shape[0], dtype=I32)[None, :]).astype(I32)
    pick = lambda a: jnp.sum(onehot * a[None, :], axis=1)
    tile = pick(first_tile) + (w - pick(item_end - per_expert))
    valid = jnp.arange(n_items, dtype=I32) < total
    lo = jnp.where(valid, jnp.maximum(pick(starts) - tile * tr, 0), 0)
    hi = jnp.where(valid, jnp.minimum(pick(ends) - tile * tr, tr), 0)
    first = jnp.concatenate([jnp.ones((1,), I32), (tile[1:] != tile[:-1]).astype(I32)])
    as_i32 = lambda a: a.astype(I32)
    return as_i32(tile), as_i32(e), as_i32(lo), as_i32(hi), as_i32(first), as_i32(total).reshape(1)


def _gmm(xs, counts, w1, b1, w2, b2):
    n_rows, d = xs.shape
    _, n_exp, _, two_ff = w1.shape
    d_ff = two_ff // 2
    tr = MOE_ROW_TILE
    n_items = n_rows // tr + n_exp - 1
    items = _group_work_items(counts, n_items)
    grid_spec = pltpu.PrefetchScalarGridSpec(
        num_scalar_prefetch=6,
        grid=(n_items,),
        in_specs=[
            pl.BlockSpec((tr, d), lambda w, tile, *_: (tile[w], 0)),
            pl.BlockSpec((None, 1, d, two_ff), lambda w, tile, exp, *_: (0, exp[w], 0, 0)),
            pl.BlockSpec((1, 1, two_ff), lambda w, tile, exp, *_: (exp[w], 0, 0)),
            pl.BlockSpec((None, 1, d_ff, d), lambda w, tile, exp, *_: (0, exp[w], 0, 0)),
            pl.BlockSpec((1, 1, d), lambda w, tile, exp, *_: (exp[w], 0, 0)),
        ],
        out_specs=pl.BlockSpec((tr, d), lambda w, tile, *_: (tile[w], 0)),
        scratch_shapes=[pltpu.VMEM((d, two_ff), BF16), pltpu.VMEM((d_ff, d), BF16)],
    )
    return pl.pallas_call(
        _gmm_body,
        grid_spec=grid_spec,
        out_shape=jax.ShapeDtypeStruct((n_rows, d), F32),
        compiler_params=_params(1),
        name="moe_gmm",
    )(*items, xs, w1, b1.reshape(n_exp, 1, two_ff), w2, b2.reshape(n_exp, 1, d))


def _combine_body(pos_ref, x2_ref, gate_ref, gain_ref, ys_ref, yp_ref, ysm_ref, buf_ref, sem, *, n_prompt_tiles):
    tm = x2_ref.shape[0]

    def issue(t, carry):
        for k in range(TOP_K):
            _row_copy(ys_ref, pos_ref[k, t], buf_ref.at[k], t, sem).start()
        return carry

    lax.fori_loop(0, tm, issue, 0)

    def drain(t, carry):
        for k in range(TOP_K):
            _row_copy(ys_ref, pos_ref[k, t], buf_ref.at[k], t, sem).wait()
        return carry

    lax.fori_loop(0, tm, drain, 0)
    gate = gate_ref[...]
    x3 = x2_ref[...]
    for k in range(TOP_K):
        x3 = x3 + gate[:, k:k + 1] * buf_ref[k]
    y = _rms(x3, gain_ref[...])
    is_prompt = pl.program_id(0) < n_prompt_tiles

    @pl.when(is_prompt)
    def _():
        yp_ref[...] = y

    @pl.when(jnp.logical_not(is_prompt))
    def _():
        ysm_ref[...] = y


def _combine(pos, x2, gate_t, gain, ys, n_prompt_rows):
    n, d = x2.shape
    tm = TOKEN_TILE
    npt = n_prompt_rows // tm
    row = lambda i: (i, 0)
    return pl.pallas_call(
        functools.partial(_combine_body, n_prompt_tiles=npt),
        grid=(n // tm,),
        in_specs=[pl.BlockSpec((TOP_K, tm), lambda i: (0, i), memory_space=pltpu.SMEM),
                  pl.BlockSpec((tm, d), row), pl.BlockSpec((tm, LANES), row), _const_spec(gain.shape),
                  pl.BlockSpec(memory_space=pl.ANY)],
        out_specs=[pl.BlockSpec((tm, d), lambda i: (jnp.minimum(i, npt - 1), 0)),
                   pl.BlockSpec((tm, d), lambda i: (jnp.maximum(i - npt, 0), 0))],
        out_shape=[jax.ShapeDtypeStruct((n_prompt_rows, d), F32),
                   jax.ShapeDtypeStruct((n - n_prompt_rows, d), F32)],
        scratch_shapes=[pltpu.VMEM((TOP_K, tm, d), F32), pltpu.SemaphoreType.DMA(())],
        compiler_params=_params(1),
        name="moe_combine",
    )(pos, x2, gate_t, gain, ys)


def _head_group_lanes(per_head, hps):
    groups = per_head.reshape(-1, 1, hps)
    padded = jnp.pad(groups, ((0, 0), (0, 0), (0, LANES - hps)))
    return jnp.broadcast_to(padded, (groups.shape[0], SUBLANES, LANES)).reshape(-1, LANES)


def kernel(x_prompt, x_sample, mem_prompt, state_gdn, state_gdn_conv, state_conf_conv, cache_mem_k, cache_mem_v,
           norm_mix, w_in, gdn_conv_w, gdn_a_log, gdn_dt_bias, gdn_norm, gdn_o, conf_conv_w, conf_conv_b,
           conf_ln_g, conf_ln_b, conf_pw2, conf_pw2_b, w_out, norm_xa, norm_mem, xa_q, xa_kv, xa_o,
           norm_ffn, router_w, router_b, moe_w1, moe_b1, moe_w2, moe_b2, norm_final):
    n_batch, seq, d = x_prompt.shape
    n_seq, n_tok, _ = x_sample.shape
    depth, _, n_heads, dk, dv = state_gdn.shape
    assert depth == 1 and dk == LANES and dv == LANES
    mem_len, xa_heads = cache_mem_k.shape[2], cache_mem_k.shape[3]
    n_experts = router_w.shape[2]
    qkv_w = gdn_conv_w.shape[2]
    v_w = n_heads * dv
    hps = GDN_HEADS_PER_STEP
    n_hg = n_heads // hps
    n_gc = gdn_conv_w.shape[1] - 1
    n_cc = conf_conv_w.shape[1] - 1
    n_p, n_s = n_batch * seq, n_seq * n_tok
    assert seq % TOKEN_TILE == 0 and n_s % TOKEN_TILE == 0 and n_seq % SEQ_BLOCK == 0
    assert n_tok * SEQ_BLOCK <= LANES and n_heads % hps == 0 and 2 * hps <= LANES
    assert n_gc <= min(n_tok, SUBLANES) and n_cc <= CONF_HALO
    row = lambda a: a.reshape(1, -1)
    to_token_major = lambda a: a.reshape(n_seq, n_tok, -1).transpose(1, 0, 2).reshape(n_s, -1)
    to_seq_major = lambda a: a.reshape(n_tok, n_seq, -1).transpose(1, 0, 2).reshape(n_s, -1)

    xp = x_prompt.reshape(n_p, d)
    xs = to_token_major(x_sample)

    w = w_in.reshape(d, -1).astype(BF16)
    ab_lo, ab_hi = qkv_w + v_w, qkv_w + v_w + 2 * n_heads
    w_a = w[:, ab_lo:ab_lo + n_heads].reshape(d, n_hg, hps)
    w_b = w[:, ab_lo + n_heads:ab_hi].reshape(d, n_hg, hps)
    wab = jnp.pad(jnp.concatenate([w_a, w_b], axis=2), ((0, 0), (0, 0), (0, LANES - 2 * hps))).reshape(d, -1)
    qkv, z, ab, cv, sg = _inproj(xp, xs, row(norm_mix), w[:, :ab_lo], wab, w[:, ab_hi:ab_hi + 2 * d],
                                 w[:, ab_hi + 2 * d:], qkv_w)

    alog = _head_group_lanes(gdn_a_log.reshape(-1), hps)
    dtb = _head_group_lanes(gdn_dt_bias.reshape(-1), hps)
    gnorm = row(gdn_norm)
    og_p, s_p, tail_q, tail_k, tail_v = _gdn_prompt(qkv, z, ab, gdn_conv_w, alog, dtb, gnorm,
                                                    n_batch, seq, n_heads, dk)
    og_s, s_s = _gdn_sample(qkv, z, ab, state_gdn_conv, state_gdn, gdn_conv_w, alog, dtb, gnorm,
                            n_p, n_seq, n_tok, n_heads, dk)

    conf_args = (conf_conv_w, row(conf_conv_b), row(conf_ln_g), row(conf_ln_b))
    yb_p, cconv_tail = _conf_prompt(cv, *conf_args, n_batch, seq)
    yb_s, cconv_s = _conf_sample(cv, state_conf_conv, *conf_args, n_p, n_seq, n_tok)

    bf = lambda a: a.reshape(a.shape[1:]).astype(BF16)
    x1, q = _merge(og_p, og_s, yb_p, yb_s, sg, xp, xs, bf(gdn_o), bf(conf_pw2), row(conf_pw2_b), bf(w_out),
                   row(norm_xa), bf(xa_q))

    mk_p, mv_p = _memkv(mem_prompt.reshape(n_batch * mem_len, d), row(norm_mem), bf(xa_kv))
    o_p = _attn_prompt(q, mk_p, mv_p, n_batch, seq, mem_len, xa_heads)
    o_s = to_token_major(_attn_sample(to_seq_major(q[n_p:]), cache_mem_k, cache_mem_v, n_tok))

    x2, hf, idx, gate_t = _router(o_p, o_s, x1, bf(xa_o), row(norm_ffn), router_w.reshape(d, n_experts).T,
                                  router_b.reshape(-1, 1))
    pos, counts = _positions(idx, n_experts)
    xsorted = _dispatch(pos, hf)
    ysorted = _gmm(xsorted, counts[:, 0], moe_w1, moe_b1, moe_w2, moe_b2)
    y_p, y_s = _combine(pos, x2, gate_t, row(norm_final), ysorted, n_p)

    y_prompt = y_p.reshape(n_batch, seq, d)
    y_sample = to_seq_major(y_s).reshape(n_seq, n_tok, d)
    gconv_p = jnp.concatenate([tail_q, tail_k, tail_v], axis=2)[:, SUBLANES - n_gc:, :]
    gconv_s = to_seq_major(qkv[n_p:]).reshape(n_seq, n_tok, qkv_w)[:, n_tok - n_gc:, :].astype(F32)
    cconv_p = cconv_tail[:, CONF_HALO - n_cc:, :]
    kv_shape = (1, n_batch, mem_len, xa_heads, d // xa_heads)
    return (y_prompt, y_sample, s_p[None], gconv_p[None], cconv_p[None], mk_p.reshape(kv_shape),
            mv_p.reshape(kv_shape), s_s[None], gconv_s[None], cconv_s[None])
```

```python
import functools

import jax
import jax.numpy as jnp
from jax import lax
from jax.experimental import pallas as pl
from jax.experimental.pallas import tpu as pltpu

F32, BF16, I32 = jnp.float32, jnp.bfloat16, jnp.int32
HIGHEST = lax.Precision.HIGHEST

LANES = 128
SUBLANES = 8
V7X_VMEM_REQUEST = 56 * 1024 * 1024

TOKEN_TILE = 512
GDN_CHUNK = 64
GDN_GROUP = 256
GDN_HEADS_PER_STEP = 4
SEQ_BLOCK = 16
CONF_HALO = 32
CONF_ROWS = 32
CONF_LANE_SPLIT = 4
CONF_NORM_ROWS = 128
MOE_ROW_TILE = 256
CUMSUM_CHUNK = 512

TOP_K = 4
SWIGLU_ALPHA = 1.702
SWIGLU_LIMIT = 7.0
RMS_EPS = 1e-6
LN_EPS = 1e-5
L2_EPS = 1e-6


def _params(n_axes):
    return pltpu.CompilerParams(dimension_semantics=("arbitrary",) * n_axes, vmem_limit_bytes=V7X_VMEM_REQUEST)


def _const_spec(shape):
    zeros = (0,) * len(shape)
    return pl.BlockSpec(shape, lambda *_: zeros, pipeline_mode=pl.Buffered(1))


def _layer_spec(shape):
    zeros = (0,) * len(shape)
    return pl.BlockSpec((None,) + tuple(shape[1:]), lambda *_: zeros, pipeline_mode=pl.Buffered(1))


def _mm(a, b):
    return jnp.dot(a.astype(BF16), b.astype(BF16), preferred_element_type=F32)


def _mm_nt(a, b):
    return lax.dot_general(a.astype(BF16), b.astype(BF16), (((1,), (1,)), ((), ())), preferred_element_type=F32)


def _rms(x, gain):
    return x * lax.rsqrt(jnp.mean(x * x, axis=-1, keepdims=True) + RMS_EPS) * gain


def _silu(x):
    return x * jax.nn.sigmoid(x)


def _softplus(x):
    return jnp.maximum(x, 0.0) + jnp.log1p(jnp.exp(-jnp.abs(x)))


def _split_hi_lo(x):
    hi = x.astype(BF16)
    lo = (x - hi.astype(F32)).astype(BF16)
    return hi, lo


def _inproj_body(xp_ref, xs_ref, gain_ref, wqkvz_ref, wab_ref, wglu_ref, wgate_ref,
                 qkv_ref, z_ref, ab_ref, cv_ref, sg_ref, *, n_prompt_tiles):
    i = pl.program_id(0)
    d = xp_ref.shape[1]
    x = jnp.where(i < n_prompt_tiles, xp_ref[...], xs_ref[...])
    h = _rms(x, gain_ref[...]).astype(BF16)
    qkv_w = qkv_ref.shape[1]
    for c in range(0, qkv_w, d):
        qkv_ref[:, c:c + d] = jnp.dot(h, wqkvz_ref[:, c:c + d], preferred_element_type=F32).astype(BF16)
    z_ref[...] = jnp.dot(h, wqkvz_ref[:, qkv_w:], preferred_element_type=F32).astype(BF16)
    ab_ref[...] = jnp.dot(h, wab_ref[...], preferred_element_type=F32)
    glu_a = jnp.dot(h, wglu_ref[:, :d], preferred_element_type=F32)
    glu_b = jnp.dot(h, wglu_ref[:, d:], preferred_element_type=F32)
    cv_ref[...] = glu_a * jax.nn.sigmoid(glu_b)
    for c in range(0, 2 * d, d):
        sg_ref[:, c:c + d] = jax.nn.sigmoid(
            jnp.dot(h, wgate_ref[:, c:c + d], preferred_element_type=F32)).astype(BF16)


def _inproj(xp, xs, gain, wqkvz, wab, wglu, wgate, qkv_w):
    n_p, d = xp.shape
    n_s = xs.shape[0]
    tm = TOKEN_TILE
    npt, nst = n_p // tm, n_s // tm
    n = n_p + n_s
    ab_w = wab.shape[1]
    row = lambda i: (i, 0)
    return pl.pallas_call(
        functools.partial(_inproj_body, n_prompt_tiles=npt),
        grid=(npt + nst,),
        in_specs=[
            pl.BlockSpec((tm, d), lambda i: (jnp.minimum(i, npt - 1), 0)),
            pl.BlockSpec((tm, d), lambda i: (jnp.maximum(i - npt, 0), 0)),
            _const_spec(gain.shape), _const_spec(wqkvz.shape), _const_spec(wab.shape),
            _const_spec(wglu.shape), _const_spec(wgate.shape),
        ],
        out_specs=[
            pl.BlockSpec((tm, qkv_w), row), pl.BlockSpec((tm, d), row), pl.BlockSpec((tm, ab_w), row),
            pl.BlockSpec((tm, d), row), pl.BlockSpec((tm, 2 * d), row),
        ],
        out_shape=[
            jax.ShapeDtypeStruct((n, qkv_w), BF16), jax.ShapeDtypeStruct((n, d), BF16),
            jax.ShapeDtypeStruct((n, ab_w), F32), jax.ShapeDtypeStruct((n, d), F32),
            jax.ShapeDtypeStruct((n, 2 * d), BF16),
        ],
        compiler_params=_params(1),
        name="inproj",
    )(xp, xs, gain, wqkvz, wab, wglu, wgate)


def _decay_scalars(ab, alog_row, dtb_row):
    return -jnp.exp(alog_row) * _softplus(ab + dtb_row), jax.nn.sigmoid(ab)


def _cumsum_rows(incl_bf16, g):
    hi, lo = _split_hi_lo(g)
    both = jnp.dot(incl_bf16, jnp.concatenate([hi, lo], axis=1), preferred_element_type=F32)
    return both[:, :LANES] + both[:, LANES:]


def _l2norm(x):
    return x * lax.rsqrt(jnp.sum(x * x, axis=-1, keepdims=True) + L2_EPS)


def _pair_masks(r, same):
    ri = lax.broadcasted_iota(I32, (r, r), 0)
    ci = lax.broadcasted_iota(I32, (r, r), 1)
    pair = same(ri, ci)
    as_f = lambda m: jnp.where(m, 1.0, 0.0).astype(F32)
    return as_f(ri == ci), as_f(pair & (ri >= ci)), -as_f(pair & (ri > ci))


def _lane_col(x, lane):
    return jnp.broadcast_to(x[:, lane:lane + 1], (x.shape[0], LANES))


def _wy_heads(q, k, v, g_all, beta_all, hps, eye, incl, neg_strict, n_doublings):
    r = q[0].shape[0]
    heads = range(len(q))
    gc_all = _cumsum_rows(incl.astype(BF16), g_all)
    pad = jnp.zeros((max(LANES - r, 0), LANES), F32)
    gc_t = jnp.transpose(jnp.concatenate([gc_all, pad], axis=0) if r < LANES else gc_all)
    gc = [_lane_col(gc_all, h) for h in heads]
    beta = [_lane_col(beta_all, hps + h) for h in heads]
    pair_decay = []
    for h in heads:
        gc_wide = jnp.concatenate([gc[h]] * (r // LANES), axis=1) if r > LANES else gc[h][:, :r]
        pair_decay.append(jnp.exp(jnp.minimum(gc_wide - gc_t[h:h + 1, :r], 0.0)))
    kb = [k[h] * beta[h] for h in heads]
    gram = [_mm_nt(jnp.concatenate([kb[h], q[h]], axis=0), k[h]) for h in heads]
    neg_m = [gram[h][:r] * (pair_decay[h] * neg_strict) for h in heads]
    p = [gram[h][r:] * (pair_decay[h] * incl) for h in heads]
    t_inv = [eye + neg_m[h] for h in heads]
    power = neg_m
    for _ in range(n_doublings):
        power = [_mm(power[h], power[h]) for h in heads]
        t_inv = [t_inv[h] + _mm(t_inv[h], power[h]) for h in heads]
    sol = [_mm(t_inv[h], jnp.concatenate([v[h] * beta[h], kb[h] * jnp.exp(gc[h])], axis=1)) for h in heads]
    return gc, [s[:, :LANES] for s in sol], [s[:, LANES:] for s in sol], p


def _gated_out(o, gnorm, z):
    on = o * lax.rsqrt(jnp.mean(o * o, axis=-1, keepdims=True) + RMS_EPS) * gnorm
    return (on * _silu(z)).astype(BF16)


def _gdn_prompt_body(q_ref, k_ref, v_ref, z_ref, ab_ref, cwq_ref, cwk_ref, cwv_ref, alog_ref, dtb_ref,
                     gnorm_ref, og_ref, sout_ref, tq_ref, tk_ref, tv_ref, halo_ref, s_ref, mask_ref, *, n_taps):
    hps, dk = s_ref.shape[0], s_ref.shape[1]
    grp, chunk = GDN_GROUP, GDN_CHUNK
    n_groups = q_ref.shape[0] // grp
    halo_rows = halo_ref.shape[1]
    halo_ref[...] = jnp.zeros(halo_ref.shape, F32)
    s_ref[...] = jnp.zeros(s_ref.shape, F32)
    for i, m in enumerate(_pair_masks(grp, lambda ri, ci: (ri // chunk) == (ci // chunk))):
        mask_ref[i] = m
    heads = range(hps)

    def group(gi, carry):
        rows = pl.ds(pl.multiple_of(gi * grp, grp), grp)
        conv = []
        for part, (x_ref, cw_ref) in enumerate(((q_ref, cwq_ref), (k_ref, cwk_ref), (v_ref, cwv_ref))):
            xg = x_ref[rows, :].astype(F32)
            xw = jnp.concatenate([halo_ref[part], xg], axis=0)
            halo_ref[part] = xg[grp - halo_rows:, :]
            y = None
            for j in range(n_taps):
                off = halo_rows - (n_taps - 1) + j
                term = cw_ref[j:j + 1, :] * xw[off:off + grp, :]
                y = term if y is None else y + term
            conv.append(_silu(y))
        sl = [slice(h * dk, (h + 1) * dk) for h in heads]
        q = [_l2norm(conv[0][:, sl[h]]) * (dk ** -0.5) for h in heads]
        k = [_l2norm(conv[1][:, sl[h]]) for h in heads]
        v = [conv[2][:, sl[h]] for h in heads]
        g_all, beta_all = _decay_scalars(ab_ref[rows, :], alog_ref[0:1, :], dtb_ref[0:1, :])
        gc, u, w, p = _wy_heads(q, k, v, g_all, beta_all, hps, mask_ref[0], mask_ref[1], mask_ref[2], 5)
        qe = [q[h] * jnp.exp(gc[h]) for h in heads]
        state = [s_ref[h] for h in heads]
        outs = [[] for _ in heads]
        for j in range(grp // chunk):
            cs = slice(j * chunk, (j + 1) * chunk)
            g_last = [gc[h][(j + 1) * chunk - 1:(j + 1) * chunk, :] for h in heads]
            k_dec = [k[h][cs] * jnp.exp(g_last[h] - gc[h][cs]) for h in heads]
            ws_qs = [_mm(jnp.concatenate([w[h][cs], qe[h][cs]], axis=0), state[h]) for h in heads]
            v_new = [u[h][cs] - ws_qs[h][:chunk] for h in heads]
            for h in heads:
                outs[h].append(ws_qs[h][chunk:] + _mm(p[h][cs, cs], v_new[h]))
            state = [state[h] * jnp.exp(g_last[h]) + lax.dot_general(
                k_dec[h].astype(BF16), v_new[h].astype(BF16), (((0,), (0,)), ((), ())),
                preferred_element_type=F32) for h in heads]
        for h in heads:
            s_ref[h] = state[h]
            o = jnp.concatenate(outs[h], axis=0)
            og_ref[rows, sl[h]] = _gated_out(o, gnorm_ref[...], z_ref[rows, sl[h]].astype(F32))
        return carry

    lax.fori_loop(0, n_groups, group, 0)
    sout_ref[0] = s_ref[...]
    tq_ref[0] = halo_ref[0]
    tk_ref[0] = halo_ref[1]
    tv_ref[0] = halo_ref[2]


def _gdn_prompt(qkv, z, ab, conv_w, alog, dtb, gnorm, n_batch, seq, n_heads, dk):
    hps = GDN_HEADS_PER_STEP
    hgw = hps * dk
    n_hg = n_heads // hps
    n_taps = conv_w.shape[1]
    grp = GDN_GROUP
    col = lambda part: (lambda b, h: (b, part * n_hg + h))
    cwcol = lambda part: (lambda b, h: (0, 0, part * n_hg + h))
    bh = lambda b, h: (b, h)
    tail = jax.ShapeDtypeStruct((n_batch, SUBLANES, n_heads * dk), F32)
    tail_spec = pl.BlockSpec((1, SUBLANES, hgw), lambda b, h: (b, 0, h))
    return pl.pallas_call(
        functools.partial(_gdn_prompt_body, n_taps=n_taps),
        grid=(n_batch, n_hg),
        in_specs=[
            pl.BlockSpec((seq, hgw), col(0)), pl.BlockSpec((seq, hgw), col(1)), pl.BlockSpec((seq, hgw), col(2)),
            pl.BlockSpec((seq, hgw), bh),
            pl.BlockSpec((seq, LANES), bh),
            pl.BlockSpec((None, n_taps, hgw), cwcol(0)), pl.BlockSpec((None, n_taps, hgw), cwcol(1)),
            pl.BlockSpec((None, n_taps, hgw), cwcol(2)),
            pl.BlockSpec((SUBLANES, LANES), lambda b, h: (h, 0)),
            pl.BlockSpec((SUBLANES, LANES), lambda b, h: (h, 0)),
            _const_spec(gnorm.shape),
        ],
        out_specs=[
            pl.BlockSpec((seq, hgw), bh),
            pl.BlockSpec((1, hps, dk, dk), lambda b, h: (b, h, 0, 0)),
            tail_spec, tail_spec, tail_spec,
        ],
        out_shape=[
            jax.ShapeDtypeStruct((n_batch * seq, n_heads * dk), BF16),
            jax.ShapeDtypeStruct((n_batch, n_heads, dk, dk), F32),
            tail, tail, tail,
        ],
        scratch_shapes=[pltpu.VMEM((3, SUBLANES, hgw), F32), pltpu.VMEM((hps, dk, dk), F32),
                        pltpu.VMEM((3, grp, grp), F32)],
        compiler_params=_params(2),
        name="gdn_prompt",
    )(qkv, qkv, qkv, z, ab, conv_w, conv_w, conv_w, alog, dtb, gnorm)


def _gdn_sample_body(*refs, n_tok, n_taps):
    sb = SEQ_BLOCK
    it = iter(refs)
    take = lambda n: [next(it) for _ in range(n)]
    x_refs = [take(n_tok) for _ in range(3)]
    cb_refs = take(3)
    z_refs = take(n_tok)
    ab_refs = take(n_tok)
    s0_ref = next(it)
    cw_refs = take(3)
    alog_ref, dtb_ref, gnorm_ref = take(3)
    og_refs = take(n_tok)
    sout_ref = next(it)

    hps, dk = s0_ref.shape[1], s0_ref.shape[2]
    heads = range(hps)
    r = n_tok * sb

    conv = []
    for part in range(3):
        ext = [cb_refs[part][:, j, :] for j in range(n_taps - 1)]
        ext += [x_refs[part][t][...].astype(F32) for t in range(n_tok)]
        ys = []
        for t in range(n_tok):
            y = None
            for j in range(n_taps):
                term = cw_refs[part][j:j + 1, :] * ext[t + j]
                y = term if y is None else y + term
            ys.append(_silu(y))
        conv.append(jnp.concatenate(ys, axis=0))
    ab = jnp.concatenate([a[...] for a in ab_refs], axis=0)

    eye, incl, neg_strict = _pair_masks(r, lambda ri, ci: (ri % sb) == (ci % sb))
    row_seq = lax.broadcasted_iota(I32, (2 * r, dk), 0) % sb
    col_seq = lax.broadcasted_iota(I32, (dk, LANES), 1) % sb
    n_doublings = max((n_tok - 1).bit_length() - 1, 0)

    sl = [slice(h * dk, (h + 1) * dk) for h in heads]
    q = [_l2norm(conv[0][:, sl[h]]) * (dk ** -0.5) for h in heads]
    k = [_l2norm(conv[1][:, sl[h]]) for h in heads]
    v = [conv[2][:, sl[h]] for h in heads]
    g_all, beta_all = _decay_scalars(ab, alog_ref[0:1, :], dtb_ref[0:1, :])
    gc, u, w, p = _wy_heads(q, k, v, g_all, beta_all, hps, eye, incl, neg_strict, n_doublings)
    for h in heads:
        wq = jnp.concatenate([w[h], q[h] * jnp.exp(gc[h])], axis=0).astype(BF16)
        ws_qs = jnp.zeros((2 * r, dk), F32)
        for s in range(sb):
            ws_qs = jnp.where(row_seq == s, _mm(wq, s0_ref[s, h]), ws_qs)
        v_new = u[h] - ws_qs[:r]
        o = ws_qs[r:] + _mm(p[h], v_new)
        z = jnp.concatenate([zr[:, sl[h]] for zr in z_refs], axis=0).astype(F32)
        og = _gated_out(o, gnorm_ref[...], z)
        for t in range(n_tok):
            og_refs[t][:, sl[h]] = og[t * sb:(t + 1) * sb]
        g_end = gc[h][r - sb:, :]
        k_dec = k[h] * jnp.exp(jnp.concatenate([g_end] * n_tok, axis=0) - gc[h])
        k_dec_t = jnp.transpose(jnp.concatenate([k_dec, jnp.zeros((LANES - r, dk), F32)], axis=0))
        v_new_pad = jnp.concatenate([v_new, jnp.zeros((LANES - r, dk), F32)], axis=0).astype(BF16)
        for s in range(sb):
            upd = _mm(jnp.where(col_seq == s, k_dec_t, 0.0), v_new_pad)
            sout_ref[s, h] = s0_ref[s, h] * jnp.exp(g_end[s:s + 1, :]) + upd


def _gdn_sample(qkv, z, ab, conv_buf, s0, conv_w, alog, dtb, gnorm, n_prompt_rows, n_seq, n_tok, n_heads, dk):
    sb, hps = SEQ_BLOCK, GDN_HEADS_PER_STEP
    hgw = hps * dk
    n_hg = n_heads // hps
    n_taps = conv_w.shape[1]
    base = n_prompt_rows // sb
    per_t = n_seq // sb

    def tok_rows(t, colfn):
        return lambda i, h: (base + t * per_t + i, colfn(h))

    in_specs, args = [], []
    for part in range(3):
        for t in range(n_tok):
            in_specs.append(pl.BlockSpec((sb, hgw), tok_rows(t, lambda h, part=part: part * n_hg + h)))
            args.append(qkv)
    for part in range(3):
        in_specs.append(pl.BlockSpec((None, sb, n_taps - 1, hgw),
                                     lambda i, h, part=part: (0, i, 0, part * n_hg + h)))
        args.append(conv_buf)
    for t in range(n_tok):
        in_specs.append(pl.BlockSpec((sb, hgw), tok_rows(t, lambda h: h)))
        args.append(z)
    for t in range(n_tok):
        in_specs.append(pl.BlockSpec((sb, LANES), tok_rows(t, lambda h: h)))
        args.append(ab)
    in_specs.append(pl.BlockSpec((None, sb, hps, dk, dk), lambda i, h: (0, i, h, 0, 0)))
    args.append(s0)
    for part in range(3):
        in_specs.append(pl.BlockSpec((None, n_taps, hgw), lambda i, h, part=part: (0, 0, part * n_hg + h)))
        args.append(conv_w)
    in_specs += [pl.BlockSpec((SUBLANES, LANES), lambda i, h: (h, 0)),
                 pl.BlockSpec((SUBLANES, LANES), lambda i, h: (h, 0)), _const_spec(gnorm.shape)]
    args += [alog, dtb, gnorm]
    outs = pl.pallas_call(
        functools.partial(_gdn_sample_body, n_tok=n_tok, n_taps=n_taps),
        grid=(per_t, n_hg),
        in_specs=in_specs,
        out_specs=[pl.BlockSpec((sb, hgw), lambda i, h: (i, h))] * n_tok
        + [pl.BlockSpec((sb, hps, dk, dk), lambda i, h: (i, h, 0, 0))],
        out_shape=[jax.ShapeDtypeStruct((n_seq, n_heads * dk), BF16)] * n_tok
        + [jax.ShapeDtypeStruct(s0.shape[1:], F32)],
        compiler_params=_params(2),
        name="gdn_sample",
    )(*args)
    return jnp.concatenate(outs[:n_tok], axis=0), outs[n_tok]


def _ln_silu(y, lg, lb):
    mu = jnp.mean(y, axis=-1, keepdims=True)
    yc = y - mu
    var = jnp.mean(yc * yc, axis=-1, keepdims=True)
    return _silu(yc * lax.rsqrt(var + LN_EPS) * lg + lb).astype(BF16)


def _conf_prompt_body(cv_ref, halo_ref, cw_ref, cb_ref, lg_ref, lb_ref, out_ref, tail_ref, rot_ref, conv_ref,
                      *, n_taps):
    tile_in_seq = pl.program_id(1)
    tt, c = cv_ref.shape
    halo = CONF_HALO
    rot_ref[0, 0:halo, :] = jnp.where(tile_in_seq == 0, 0.0, halo_ref[...])
    rot_ref[0, halo:halo + tt, :] = cv_ref[...]
    rot_ref[0, halo + tt:halo + tt + SUBLANES, :] = jnp.zeros((SUBLANES, c), F32)
    step = 32

    def rotate(i, carry):
        base = pl.multiple_of(i * step, step)
        win = rot_ref[0, pl.ds(base, step + SUBLANES), :]
        for r in range(1, SUBLANES):
            rot_ref[r, pl.ds(base, step), :] = win[r:r + step, :]
        return carry

    lax.fori_loop(0, (halo + tt) // step, rotate, 0)
    lead = halo - (n_taps - 1)
    rows = CONF_ROWS
    groups = rows // SUBLANES
    lane_w = c // CONF_LANE_SPLIT
    by_rot = [[(j, (lead + j) // SUBLANES) for j in range(n_taps) if (lead + j) % SUBLANES == r]
              for r in range(SUBLANES)]

    for lh in range(CONF_LANE_SPLIT):
        ls = slice(lh * lane_w, (lh + 1) * lane_w)

        def conv(i, carry, ls=ls):
            base = pl.multiple_of(i * rows, rows)
            accs = [jnp.broadcast_to(cb_ref[:, ls], (SUBLANES, lane_w))] * groups
            for r, taps in enumerate(by_rot):
                if not taps:
                    continue
                lo = min(a for _, a in taps)
                hi = max(a for _, a in taps) + groups
                slabs = {m: rot_ref[r, pl.ds(pl.multiple_of(base + m * SUBLANES, SUBLANES), SUBLANES), ls]
                         for m in range(lo, hi)}
                for j, a in taps:
                    w8 = cw_ref[j, :, ls]
                    accs = [acc + w8 * slabs[a + g] for g, acc in enumerate(accs)]
            conv_ref[pl.ds(base, rows), ls] = jnp.concatenate(accs, axis=0)
            return carry

        lax.fori_loop(0, tt // rows, conv, 0)

    nrows = CONF_NORM_ROWS

    def norm(i, carry):
        base = pl.multiple_of(i * nrows, nrows)
        out_ref[pl.ds(base, nrows), :] = _ln_silu(conv_ref[pl.ds(base, nrows), :], lg_ref[...], lb_ref[...])
        return carry

    lax.fori_loop(0, tt // nrows, norm, 0)

    @pl.when(tile_in_seq == pl.num_programs(1) - 1)
    def _():
        tail_ref[0] = cv_ref[tt - halo:, :]


def _conf_prompt(cv, cw, cb, lg, lb, n_batch, seq):
    c = cv.shape[1]
    tt = TOKEN_TILE
    tiles = seq // tt
    halo_per_tile = tt // CONF_HALO
    n_taps = cw.shape[1]
    cw8 = jnp.broadcast_to(cw.reshape(n_taps, 1, c), (n_taps, SUBLANES, c))
    return pl.pallas_call(
        functools.partial(_conf_prompt_body, n_taps=n_taps),
        grid=(n_batch, tiles),
        in_specs=[
            pl.BlockSpec((tt, c), lambda b, j: (b * tiles + j, 0)),
            pl.BlockSpec((CONF_HALO, c), lambda b, j: (jnp.maximum((b * tiles + j) * halo_per_tile - 1, 0), 0)),
            _const_spec(cw8.shape), _const_spec(cb.shape), _const_spec(lg.shape), _const_spec(lb.shape),
        ],
        out_specs=[pl.BlockSpec((tt, c), lambda b, j: (b * tiles + j, 0)),
                   pl.BlockSpec((1, CONF_HALO, c), lambda b, j: (b, 0, 0))],
        out_shape=[jax.ShapeDtypeStruct((n_batch * seq, c), BF16),
                   jax.ShapeDtypeStruct((n_batch, CONF_HALO, c), F32)],
        scratch_shapes=[pltpu.VMEM((SUBLANES, CONF_HALO + tt + SUBLANES, c), F32), pltpu.VMEM((tt, c), F32)],
        compiler_params=_params(2),
        name="conf_prompt",
    )(cv, cv, cw8, cb, lg, lb)


def _conf_sample_body(*refs, n_tok, n_taps):
    buf_ref = refs[0]
    cv_refs = refs[1:1 + n_tok]
    cw_ref, cb_ref, lg_ref, lb_ref = refs[1 + n_tok:5 + n_tok]
    out_refs = refs[5 + n_tok:5 + 2 * n_tok]
    newbuf_ref = refs[5 + 2 * n_tok]
    n_hist = n_taps - 1

    def ext(i):
        return buf_ref[:, i, :] if i < n_hist else cv_refs[i - n_hist][...]

    for t in range(n_tok):
        acc = jnp.broadcast_to(cb_ref[...], cv_refs[0].shape)
        for j in range(n_taps):
            acc = acc + cw_ref[j:j + 1, :] * ext(t + j)
        out_refs[t][...] = _ln_silu(acc, lg_ref[...], lb_ref[...])
    for i in range(n_hist):
        newbuf_ref[:, i, :] = ext(i + n_tok)


def _conf_sample(cv, buf, cw, cb, lg, lb, n_prompt_rows, n_seq, n_tok):
    sb = SEQ_BLOCK
    c = cv.shape[1]
    n_taps = cw.shape[1]
    base, per_t = n_prompt_rows // sb, n_seq // sb
    in_specs = [pl.BlockSpec((None, sb, n_taps - 1, c), lambda i: (0, i, 0, 0))]
    in_specs += [pl.BlockSpec((sb, c), lambda i, t=t: (base + t * per_t + i, 0)) for t in range(n_tok)]
    in_specs += [_layer_spec(cw.shape), _const_spec(cb.shape), _const_spec(lg.shape), _const_spec(lb.shape)]
    outs = pl.pallas_call(
        functools.partial(_conf_sample_body, n_tok=n_tok, n_taps=n_taps),
        grid=(per_t,),
        in_specs=in_specs,
        out_specs=[pl.BlockSpec((sb, c), lambda i: (i, 0))] * n_tok
        + [pl.BlockSpec((sb, n_taps - 1, c), lambda i: (i, 0, 0))],
        out_shape=[jax.ShapeDtypeStruct((n_seq, c), BF16)] * n_tok
        + [jax.ShapeDtypeStruct((n_seq, n_taps - 1, c), F32)],
        compiler_params=_params(1),
        name="conf_sample",
    )(buf, *([cv] * n_tok), cw, cb, lg, lb)
    return jnp.concatenate(outs[:n_tok], axis=0), outs[n_tok]


def _merge_body(ogp_ref, ogs_ref, ybp_ref, ybs_ref, sg_ref, xp_ref, xs_ref, wgo_ref, wpw_ref, bpw_ref,
                wout_ref, gxa_ref, wq_ref, x1_ref, q_ref, *, n_prompt_tiles):
    is_prompt = pl.program_id(0) < n_prompt_tiles
    d = xp_ref.shape[1]
    og = jnp.where(is_prompt, ogp_ref[...], ogs_ref[...])
    yb_in = jnp.where(is_prompt, ybp_ref[...], ybs_ref[...])
    x = jnp.where(is_prompt, xp_ref[...], xs_ref[...])
    y_a = jnp.dot(og, wgo_ref[...], preferred_element_type=F32)
    y_b = jnp.dot(yb_in, wpw_ref[...], preferred_element_type=F32) + bpw_ref[...]
    mixed = sg_ref[:, :d].astype(F32) * y_a + sg_ref[:, d:].astype(F32) * y_b
    x1 = x + jnp.dot(mixed.astype(BF16), wout_ref[...], preferred_element_type=F32)
    x1_ref[...] = x1
    q_ref[...] = jnp.dot(_rms(x1, gxa_ref[...]).astype(BF16), wq_ref[...],
                         preferred_element_type=F32).astype(BF16)


def _dual(tm, cols, npt):
    return [pl.BlockSpec((tm, cols), lambda i: (jnp.minimum(i, npt - 1), 0)),
            pl.BlockSpec((tm, cols), lambda i: (jnp.maximum(i - npt, 0), 0))]


def _merge(og_p, og_s, yb_p, yb_s, sg, xp, xs, wgo, wpw, bpw, wout, gxa, wq):
    n_p, d = xp.shape
    tm = TOKEN_TILE
    npt, nst = n_p // tm, xs.shape[0] // tm
    n = n_p + xs.shape[0]
    row = lambda i: (i, 0)
    return pl.pallas_call(
        functools.partial(_merge_body, n_prompt_tiles=npt),
        grid=(npt + nst,),
        in_specs=_dual(tm, d, npt) + _dual(tm, d, npt) + [pl.BlockSpec((tm, 2 * d), row)] + _dual(tm, d, npt)
        + [_const_spec(a.shape) for a in (wgo, wpw, bpw, wout, gxa, wq)],
        out_specs=[pl.BlockSpec((tm, d), row), pl.BlockSpec((tm, d), row)],
        out_shape=[jax.ShapeDtypeStruct((n, d), F32), jax.ShapeDtypeStruct((n, d), BF16)],
        compiler_params=_params(1),
        name="merge",
    )(og_p, og_s, yb_p, yb_s, sg, xp, xs, wgo, wpw, bpw, wout, gxa, wq)


def _memkv_body(m_ref, gain_ref, w_ref, k_ref, v_ref):
    d = k_ref.shape[1]
    h = _rms(m_ref[...], gain_ref[...]).astype(BF16)
    k_ref[...] = jnp.dot(h, w_ref[:, :d], preferred_element_type=F32)
    v_ref[...] = jnp.dot(h, w_ref[:, d:], preferred_element_type=F32)


def _memkv(mem, gain, w):
    n, d = mem.shape
    tm = min(TOKEN_TILE, n)
    row = lambda i: (i, 0)
    return pl.pallas_call(
        _memkv_body,
        grid=(n // tm,),
        in_specs=[pl.BlockSpec((tm, d), row), _const_spec(gain.shape), _const_spec(w.shape)],
        out_specs=[pl.BlockSpec((tm, d), row)] * 2,
        out_shape=[jax.ShapeDtypeStruct((n, d), F32)] * 2,
        compiler_params=_params(1),
        name="memkv",
    )(mem, gain, w)


def _softmax_rows(s):
    e = jnp.exp(s - jnp.max(s, axis=-1, keepdims=True))
    return e / jnp.sum(e, axis=-1, keepdims=True)


def _attn_prompt_body(q_ref, k_ref, v_ref, o_ref, *, n_heads):
    dh = q_ref.shape[1] // n_heads
    for h in range(n_heads):
        sl = slice(h * dh, (h + 1) * dh)
        s = _mm_nt(q_ref[:, sl], k_ref[:, sl]) * (dh ** -0.5)
        o_ref[:, sl] = _mm(_softmax_rows(s), v_ref[:, sl]).astype(BF16)


def _attn_prompt(q, k, v, n_batch, seq, mem_len, n_heads):
    d = q.shape[1]
    tq = TOKEN_TILE
    tiles = seq // tq
    return pl.pallas_call(
        functools.partial(_attn_prompt_body, n_heads=n_heads),
        grid=(n_batch, tiles),
        in_specs=[
            pl.BlockSpec((tq, d), lambda b, j: (b * tiles + j, 0)),
            pl.BlockSpec((mem_len, d), lambda b, j: (b, 0)),
            pl.BlockSpec((mem_len, d), lambda b, j: (b, 0)),
        ],
        out_specs=pl.BlockSpec((tq, d), lambda b, j: (b * tiles + j, 0)),
        out_shape=jax.ShapeDtypeStruct((n_batch * seq, d), BF16),
        compiler_params=_params(2),
        name="attn_prompt",
    )(q, k, v)


def _attn_sample_body(q_ref, k_ref, v_ref, o_ref, *, n_heads):
    sb, mem_len, d = k_ref.shape
    dh = d // n_heads
    rows_per_seq = q_ref.shape[0] // sb
    for h in range(n_heads):
        sl = slice(h * dh, (h + 1) * dh)
        keys = k_ref[:, :, sl].reshape(sb * mem_len, dh)
        vals = v_ref[:, :, sl].reshape(sb * mem_len, dh)
        s = _mm_nt(q_ref[:, sl], keys) * (dh ** -0.5)
        row_seq = lax.broadcasted_iota(I32, s.shape, 0) // rows_per_seq
        col_seq = lax.broadcasted_iota(I32, s.shape, 1) // mem_len
        p = _softmax_rows(jnp.where(row_seq == col_seq, s, -jnp.inf))
        o_ref[:, sl] = _mm(p, vals).astype(BF16)


def _attn_sample(q_seq_major, kc, vc, n_tok, n_heads):
    n_s, d = q_seq_major.shape
    sb = SEQ_BLOCK
    rows = sb * n_tok
    cache_spec = pl.BlockSpec((sb,) + kc.shape[1:], lambda i: (i, 0, 0))
    return pl.pallas_call(
        functools.partial(_attn_sample_body, n_heads=n_heads),
        grid=(n_s // rows,),
        in_specs=[pl.BlockSpec((rows, d), lambda i: (i, 0)), cache_spec, cache_spec],
        out_specs=pl.BlockSpec((rows, d), lambda i: (i, 0)),
        out_shape=jax.ShapeDtypeStruct((n_s, d), BF16),
        compiler_params=_params(1),
        name="attn_sample",
    )(q_seq_major, kc, vc)


def _router_body(op_ref, os_ref, x1_ref, wo_ref, gain_ref, rwt_ref, rb_ref,
                 x2_ref, hf_ref, idx_ref, gate_ref, *, n_prompt_tiles):
    o = jnp.where(pl.program_id(0) < n_prompt_tiles, op_ref[...], os_ref[...])
    x2 = x1_ref[...] + jnp.dot(o, wo_ref[...], preferred_element_type=F32)
    x2_ref[...] = x2
    hf = _rms(x2, gain_ref[...])
    hf_ref[...] = hf
    logits = lax.dot_general(rwt_ref[...], hf, (((1,), (1,)), ((), ())), precision=HIGHEST,
                             preferred_element_type=F32) + rb_ref[...]
    n_exp, tm = logits.shape
    expert = lax.broadcasted_iota(I32, logits.shape, 0)
    vals, picks = [], []
    for _ in range(TOP_K):
        best = jnp.max(logits, axis=0, keepdims=True)
        pick = jnp.min(jnp.where(logits == best, expert, n_exp), axis=0, keepdims=True)
        vals.append(best)
        picks.append(pick)
        logits = jnp.where(expert == pick, -jnp.inf, logits)
    idx_ref[...] = jnp.concatenate(picks, axis=0)
    es = [jnp.exp(v - vals[0]) for v in vals]
    total = functools.reduce(lambda a, b: a + b, es)
    gates = jnp.concatenate([e / total for e in es] + [jnp.zeros((LANES - TOP_K, tm), F32)], axis=0)
    gate_ref[...] = jnp.transpose(gates)


def _router(o_p, o_s, x1, wo, gain, rwt, rb):
    n, d = x1.shape
    tm = TOKEN_TILE
    npt = o_p.shape[0] // tm
    row = lambda i: (i, 0)
    return pl.pallas_call(
        functools.partial(_router_body, n_prompt_tiles=npt),
        grid=(n // tm,),
        in_specs=_dual(tm, d, npt) + [pl.BlockSpec((tm, d), row)]
        + [_const_spec(a.shape) for a in (wo, gain, rwt, rb)],
        out_specs=[pl.BlockSpec((tm, d), row), pl.BlockSpec((tm, d), row),
                   pl.BlockSpec((TOP_K, tm), lambda i: (0, i)), pl.BlockSpec((tm, LANES), row)],
        out_shape=[jax.ShapeDtypeStruct((n, d), F32), jax.ShapeDtypeStruct((n, d), F32),
                   jax.ShapeDtypeStruct((TOP_K, n), I32), jax.ShapeDtypeStruct((n, LANES), F32)],
        compiler_params=_params(1),
        name="router",
    )(o_p, o_s, x1, wo, gain, rwt, rb)


def _positions_body(idx_ref, pos_ref, counts_ref, rank_ref, *, n_experts):
    ch = CUMSUM_CHUNK
    n = idx_ref.shape[1]
    n_chunks = n // ch
    upper = (lax.broadcasted_iota(I32, (ch, ch), 0) <= lax.broadcasted_iota(I32, (ch, ch), 1)).astype(BF16)
    expert = lax.broadcasted_iota(I32, (n_experts, ch), 0)

    def onehot(k, c):
        cols = pl.ds(pl.multiple_of(c * ch, ch), ch)
        return cols, expert == idx_ref[k:k + 1, cols]

    running = jnp.zeros((n_experts, 1), F32)
    for k in range(TOP_K):
        def count(c, run, k=k):
            cols, oh = onehot(k, c)
            cum = jnp.dot(oh.astype(BF16), upper, preferred_element_type=F32)
            rank_ref[k:k + 1, cols] = jnp.sum(jnp.where(oh, cum - 1.0 + run, 0.0), axis=0, keepdims=True)
            return run + cum[:, ch - 1:ch]
        running = lax.fori_loop(0, n_chunks, count, running)

    counts = jnp.broadcast_to(running, (n_experts, LANES))
    counts_ref[...] = counts.astype(I32)
    lower = (lax.broadcasted_iota(I32, (n_experts, n_experts), 0)
             > lax.broadcasted_iota(I32, (n_experts, n_experts), 1)).astype(F32)
    start = jnp.dot(lower, counts, precision=HIGHEST, preferred_element_type=F32)[:, 0:1]

    for k in range(TOP_K):
        def place(c, carry, k=k):
            cols, oh = onehot(k, c)
            first = jnp.sum(jnp.where(oh, start, 0.0), axis=0, keepdims=True)
            pos_ref[k:k + 1, cols] = (rank_ref[k:k + 1, cols] + first).astype(I32)
            return carry
        lax.fori_loop(0, n_chunks, place, 0)


def _positions(idx, n_experts):
    k, n = idx.shape
    return pl.pallas_call(
        functools.partial(_positions_body, n_experts=n_experts),
        out_shape=[jax.ShapeDtypeStruct((k, n), I32), jax.ShapeDtypeStruct((n_experts, LANES), I32)],
        scratch_shapes=[pltpu.VMEM((k, n), F32)],
        compiler_params=pltpu.CompilerParams(vmem_limit_bytes=V7X_VMEM_REQUEST),
        name="positions",
    )(idx)


def _row_copy(src_ref, src_row, dst_ref, dst_row, sem):
    return pltpu.make_async_copy(src_ref.at[pl.ds(src_row, 1), :], dst_ref.at[pl.ds(dst_row, 1), :], sem)


def _dispatch_body(pos_ref, hf_ref, xs_ref, sem):
    tm = hf_ref.shape[0]

    def issue(t, carry):
        for k in range(TOP_K):
            _row_copy(hf_ref, t, xs_ref, pos_ref[k, t], sem).start()
        return carry

    lax.fori_loop(0, tm, issue, 0)

    def drain(t, carry):
        for k in range(TOP_K):
            _row_copy(hf_ref, t, xs_ref, pos_ref[k, t], sem).wait()
        return carry

    lax.fori_loop(0, tm, drain, 0)


def _dispatch(pos, hf):
    n, d = hf.shape
    tm = TOKEN_TILE
    return pl.pallas_call(
        _dispatch_body,
        grid=(n // tm,),
        in_specs=[pl.BlockSpec((TOP_K, tm), lambda i: (0, i), memory_space=pltpu.SMEM),
                  pl.BlockSpec((tm, d), lambda i: (i, 0))],
        out_specs=pl.BlockSpec(memory_space=pl.ANY),
        out_shape=jax.ShapeDtypeStruct((TOP_K * n, d), F32),
        scratch_shapes=[pltpu.SemaphoreType.DMA(())],
        compiler_params=_params(1),
        name="dispatch",
    )(pos, hf)


def _gmm_body(tile_ref, exp_ref, lo_ref, hi_ref, first_ref, nitems_ref,
              xs_ref, w1_ref, b1_ref, w2_ref, b2_ref, ys_ref, w1b_ref, w2b_ref):
    w = pl.program_id(0)

    @pl.when(w < nitems_ref[0])
    def _():
        new_expert = jnp.logical_or(w == 0, exp_ref[w] != exp_ref[jnp.maximum(w - 1, 0)])

        @pl.when(new_expert)
        def _():
            w1b_ref[...] = w1_ref[0].astype(BF16)
            w2b_ref[...] = w2_ref[0].astype(BF16)

        d_ff = w2b_ref.shape[0]
        hid = jnp.dot(xs_ref[...].astype(BF16), w1b_ref[...], preferred_element_type=F32) + b1_ref[0]
        glu = jnp.minimum(hid[:, :d_ff], SWIGLU_LIMIT)
        lin = jnp.clip(hid[:, d_ff:], -SWIGLU_LIMIT, SWIGLU_LIMIT)
        act = glu * jax.nn.sigmoid(SWIGLU_ALPHA * glu) * (lin + 1.0)
        y = jnp.dot(act.astype(BF16), w2b_ref[...], preferred_element_type=F32) + b2_ref[0]
        rows = lax.broadcasted_iota(I32, (ys_ref.shape[0], 1), 0)
        y = jnp.where((rows >= lo_ref[w]) & (rows < hi_ref[w]), y, 0.0)

        @pl.when(first_ref[w] == 1)
        def _():
            ys_ref[...] = y

        @pl.when(first_ref[w] == 0)
        def _():
            ys_ref[...] += y


def _group_work_items(counts, n_items):
    tr = MOE_ROW_TILE
    ends = jnp.cumsum(counts)
    starts = ends - counts
    first_tile = starts // tr
    last_tile = jnp.maximum(ends - 1, 0) // tr
    per_expert = jnp.where(counts > 0, last_tile - first_tile + 1, 0)
    item_end = jnp.cumsum(per_expert)
    total = item_end[-1]
    w = jnp.minimum(jnp.arange(n_items, dtype=I32), total - 1)
    e = jnp.sum((item_end[None, :] <= w[:, None]).astype(I32), axis=1)
    onehot = (e[:, None] == jnp.arange(counts.shape[0], dtype=I32)[None, :]).astype(I32)
    pick = lambda a: jnp.sum(onehot * a[None, :], axis=1)
    tile = pick(first_tile) + (w - pick(item_end - per_expert))
    valid = jnp.arange(n_items, dtype=I32) < total
    lo = jnp.where(valid, jnp.maximum(pick(starts) - tile * tr, 0), 0)
    hi = jnp.where(valid, jnp.minimum(pick(ends) - tile * tr, tr), 0)
    first = jnp.concatenate([jnp.ones((1,), I32), (tile[1:] != tile[:-1]).astype(I32)])
    as_i32 = lambda a: a.astype(I32)
    return as_i32(tile), as_i32(e), as_i32(lo), as_i32(hi), as_i32(first), as_i32(total).reshape(1)


def _gmm(xs, counts, w1, b1, w2, b2):
    n_rows, d = xs.shape
    _, n_exp, _, two_ff = w1.shape
    d_ff = two_ff // 2
    tr = MOE_ROW_TILE
    n_items = n_rows // tr + n_exp - 1
    items = _group_work_items(counts, n_items)
    grid_spec = pltpu.PrefetchScalarGridSpec(
        num_scalar_prefetch=6,
        grid=(n_items,),
        in_specs=[
            pl.BlockSpec((tr, d), lambda w, tile, *_: (tile[w], 0)),
            pl.BlockSpec((None, 1, d, two_ff), lambda w, tile, exp, *_: (0, exp[w], 0, 0)),
            pl.BlockSpec((1, 1, two_ff), lambda w, tile, exp, *_: (exp[w], 0, 0)),
            pl.BlockSpec((None, 1, d_ff, d), lambda w, tile, exp, *_: (0, exp[w], 0, 0)),
            pl.BlockSpec((1, 1, d), lambda w, tile, exp, *_: (exp[w], 0, 0)),
        ],
        out_specs=pl.BlockSpec((tr, d), lambda w, tile, *_: (tile[w], 0)),
        scratch_shapes=[pltpu.VMEM((d, two_ff), BF16), pltpu.VMEM((d_ff, d), BF16)],
    )
    return pl.pallas_call(
        _gmm_body,
        grid_spec=grid_spec,
        out_shape=jax.ShapeDtypeStruct((n_rows, d), F32),
        compiler_params=_params(1),
        name="moe_gmm",
    )(*items, xs, w1, b1.reshape(n_exp, 1, two_ff), w2, b2.reshape(n_exp, 1, d))


def _combine_body(pos_ref, x2_ref, gate_ref, gain_ref, ys_ref, yp_ref, ysm_ref, buf_ref, sem, *, n_prompt_tiles):
    tm = x2_ref.shape[0]

    def issue(t, carry):
        for k in range(TOP_K):
            _row_copy(ys_ref, pos_ref[k, t], buf_ref.at[k], t, sem).start()
        return carry

    lax.fori_loop(0, tm, issue, 0)

    def drain(t, carry):
        for k in range(TOP_K):
            _row_copy(ys_ref, pos_ref[k, t], buf_ref.at[k], t, sem).wait()
        return carry

    lax.fori_loop(0, tm, drain, 0)
    gate = gate_ref[...]
    x3 = x2_ref[...]
    for k in range(TOP_K):
        x3 = x3 + gate[:, k:k + 1] * buf_ref[k]
    y = _rms(x3, gain_ref[...])
    is_prompt = pl.program_id(0) < n_prompt_tiles

    @pl.when(is_prompt)
    def _():
        yp_ref[...] = y

    @pl.when(jnp.logical_not(is_prompt))
    def _():
        ysm_ref[...] = y


def _combine(pos, x2, gate_t, gain, ys, n_prompt_rows):
    n, d = x2.shape
    tm = TOKEN_TILE
    npt = n_prompt_rows // tm
    row = lambda i: (i, 0)
    return pl.pallas_call(
        functools.partial(_combine_body, n_prompt_tiles=npt),
        grid=(n // tm,),
        in_specs=[pl.BlockSpec((TOP_K, tm), lambda i: (0, i), memory_space=pltpu.SMEM),
                  pl.BlockSpec((tm, d), row), pl.BlockSpec((tm, LANES), row), _const_spec(gain.shape),
                  pl.BlockSpec(memory_space=pl.ANY)],
        out_specs=[pl.BlockSpec((tm, d), lambda i: (jnp.minimum(i, npt - 1), 0)),
                   pl.BlockSpec((tm, d), lambda i: (jnp.maximum(i - npt, 0), 0))],
        out_shape=[jax.ShapeDtypeStruct((n_prompt_rows, d), F32),
                   jax.ShapeDtypeStruct((n - n_prompt_rows, d), F32)],
        scratch_shapes=[pltpu.VMEM((TOP_K, tm, d), F32), pltpu.SemaphoreType.DMA(())],
        compiler_params=_params(1),
        name="moe_combine",
    )(pos, x2, gate_t, gain, ys)


def _head_group_lanes(per_head, hps):
    groups = per_head.reshape(-1, 1, hps)
    padded = jnp.pad(groups, ((0, 0), (0, 0), (0, LANES - hps)))
    return jnp.broadcast_to(padded, (groups.shape[0], SUBLANES, LANES)).reshape(-1, LANES)


def kernel(x_prompt, x_sample, mem_prompt, state_gdn, state_gdn_conv, state_conf_conv, cache_mem_k, cache_mem_v,
           norm_mix, w_in, gdn_conv_w, gdn_a_log, gdn_dt_bias, gdn_norm, gdn_o, conf_conv_w, conf_conv_b,
           conf_ln_g, conf_ln_b, conf_pw2, conf_pw2_b, w_out, norm_xa, norm_mem, xa_q, xa_kv, xa_o,
           norm_ffn, router_w, router_b, moe_w1, moe_b1, moe_w2, moe_b2, norm_final):
    n_batch, seq, d = x_prompt.shape
    n_seq, n_tok, _ = x_sample.shape
    depth, _, n_heads, dk, dv = state_gdn.shape
    assert depth == 1 and dk == LANES and dv == LANES
    mem_len, xa_heads = cache_mem_k.shape[2], cache_mem_k.shape[3]
    n_experts = router_w.shape[2]
    qkv_w = gdn_conv_w.shape[2]
    v_w = n_heads * dv
    hps = GDN_HEADS_PER_STEP
    n_hg = n_heads // hps
    n_gc = gdn_conv_w.shape[1] - 1
    n_cc = conf_conv_w.shape[1] - 1
    n_p, n_s = n_batch * seq, n_seq * n_tok
    assert seq % TOKEN_TILE == 0 and n_s % TOKEN_TILE == 0 and n_seq % SEQ_BLOCK == 0
    assert n_tok * SEQ_BLOCK <= LANES and n_heads % hps == 0 and 2 * hps <= LANES
    assert n_gc <= min(n_tok, SUBLANES) and n_cc <= CONF_HALO
    row = lambda a: a.reshape(1, -1)
    to_token_major = lambda a: a.reshape(n_seq, n_tok, -1).transpose(1, 0, 2).reshape(n_s, -1)
    to_seq_major = lambda a: a.reshape(n_tok, n_seq, -1).transpose(1, 0, 2).reshape(n_s, -1)

    xp = x_prompt.reshape(n_p, d)
    xs = to_token_major(x_sample)

    w = w_in.reshape(d, -1).astype(BF16)
    ab_lo, ab_hi = qkv_w + v_w, qkv_w + v_w + 2 * n_heads
    w_a = w[:, ab_lo:ab_lo + n_heads].reshape(d, n_hg, hps)
    w_b = w[:, ab_lo + n_heads:ab_hi].reshape(d, n_hg, hps)
    wab = jnp.pad(jnp.concatenate([w_a, w_b], axis=2), ((0, 0), (0, 0), (0, LANES - 2 * hps))).reshape(d, -1)
    qkv, z, ab, cv, sg = _inproj(xp, xs, row(norm_mix), w[:, :ab_lo], wab, w[:, ab_hi:ab_hi + 2 * d],
                                 w[:, ab_hi + 2 * d:], qkv_w)

    alog = _head_group_lanes(gdn_a_log.reshape(-1), hps)
    dtb = _head_group_lanes(gdn_dt_bias.reshape(-1), hps)
    gnorm = row(gdn_norm)
    og_p, s_p, tail_q, tail_k, tail_v = _gdn_prompt(qkv, z, ab, gdn_conv_w, alog, dtb, gnorm,
                                                    n_batch, seq, n_heads, dk)
    og_s, s_s = _gdn_sample(qkv, z, ab, state_gdn_conv, state_gdn, gdn_conv_w, alog, dtb, gnorm,
                            n_p, n_seq, n_tok, n_heads, dk)

    conf_args = (conf_conv_w, row(conf_conv_b), row(conf_ln_g), row(conf_ln_b))
    yb_p, cconv_tail = _conf_prompt(cv, *conf_args, n_batch, seq)
    yb_s, cconv_s = _conf_sample(cv, state_conf_conv, *conf_args, n_p, n_seq, n_tok)

    bf = lambda a: a.reshape(a.shape[1:]).astype(BF16)
    x1, q = _merge(og_p, og_s, yb_p, yb_s, sg, xp, xs, bf(gdn_o), bf(conf_pw2), row(conf_pw2_b), bf(w_out),
                   row(norm_xa), bf(xa_q))

    mk_p, mv_p = _memkv(mem_prompt.reshape(n_batch * mem_len, d), row(norm_mem), bf(xa_kv))
    o_p = _attn_prompt(q, mk_p, mv_p, n_batch, seq, mem_len, xa_heads)
    cache_bf = lambda c: c.reshape(n_seq, mem_len, d).astype(BF16)
    o_s = to_token_major(_attn_sample(to_seq_major(q[n_p:]), cache_bf(cache_mem_k), cache_bf(cache_mem_v),
                                      n_tok, xa_heads))

    x2, hf, idx, gate_t = _router(o_p, o_s, x1, bf(xa_o), row(norm_ffn), router_w.reshape(d, n_experts).T,
                                  router_b.reshape(-1, 1))
    pos, counts = _positions(idx, n_experts)
    xsorted = _dispatch(pos, hf)
    ysorted = _gmm(xsorted, counts[:, 0], moe_w1, moe_b1, moe_w2, moe_b2)
    y_p, y_s = _combine(pos, x2, gate_t, row(norm_final), ysorted, n_p)

    y_prompt = y_p.reshape(n_batch, seq, d)
    y_sample = to_seq_major(y_s).reshape(n_seq, n_tok, d)
    gconv_p = jnp.concatenate([tail_q, tail_k, tail_v], axis=2)[:, SUBLANES - n_gc:, :]
    gconv_s = to_seq_major(qkv[n_p:]).reshape(n_seq, n_tok, qkv_w)[:, n_tok - n_gc:, :].astype(F32)
    cconv_p = cconv_tail[:, CONF_HALO - n_cc:, :]
    kv_shape = (1, n_batch, mem_len, xa_heads, d // xa_heads)
    return (y_prompt, y_sample, s_p[None], gconv_p[None], cconv_p[None], mk_p.reshape(kv_shape),
            mv_p.reshape(kv_shape), s_s[None], gconv_s[None], cconv_s[None])
```

```python
import functools

import jax
import jax.numpy as jnp
from jax import lax
from jax.experimental import pallas as pl
from jax.experimental.pallas import tpu as pltpu

F32, BF16, I32 = jnp.float32, jnp.bfloat16, jnp.int32
HIGHEST = lax.Precision.HIGHEST

LANES = 128
SUBLANES = 8
V7X_VMEM_REQUEST = 56 * 1024 * 1024

TOKEN_TILE = 512
GDN_CHUNK = 64
GDN_GROUP = 256
GDN_HEADS_PER_STEP = 4
SEQ_BLOCK = 16
CONF_HALO = 32
CONF_ROWS = 32
CONF_LANE_SPLIT = 4
CONF_NORM_ROWS = 128
MOE_ROW_TILE = 256
CUMSUM_CHUNK = 512

TOP_K = 4
SWIGLU_ALPHA = 1.702
SWIGLU_LIMIT = 7.0
RMS_EPS = 1e-6
LN_EPS = 1e-5
L2_EPS = 1e-6


def _params(n_axes):
    return pltpu.CompilerParams(dimension_semantics=("arbitrary",) * n_axes, vmem_limit_bytes=V7X_VMEM_REQUEST)


def _const_spec(shape):
    zeros = (0,) * len(shape)
    return pl.BlockSpec(shape, lambda *_: zeros, pipeline_mode=pl.Buffered(1))


def _layer_spec(shape):
    zeros = (0,) * len(shape)
    return pl.BlockSpec((None,) + tuple(shape[1:]), lambda *_: zeros, pipeline_mode=pl.Buffered(1))


def _mm(a, b):
    return jnp.dot(a.astype(BF16), b.astype(BF16), preferred_element_type=F32)


def _mm_nt(a, b):
    return lax.dot_general(a.astype(BF16), b.astype(BF16), (((1,), (1,)), ((), ())), preferred_element_type=F32)


def _rms(x, gain):
    return x * lax.rsqrt(jnp.mean(x * x, axis=-1, keepdims=True) + RMS_EPS) * gain


def _silu(x):
    return x * jax.nn.sigmoid(x)


def _softplus(x):
    return jnp.maximum(x, 0.0) + jnp.log1p(jnp.exp(-jnp.abs(x)))


def _split_hi_lo(x):
    hi = x.astype(BF16)
    lo = (x - hi.astype(F32)).astype(BF16)
    return hi, lo


def _inproj_body(xp_ref, xs_ref, gain_ref, wqkvz_ref, wab_ref, wglu_ref, wgate_ref,
                 qkv_ref, z_ref, ab_ref, cv_ref, sg_ref, *, n_prompt_tiles):
    i = pl.program_id(0)
    d = xp_ref.shape[1]
    x = jnp.where(i < n_prompt_tiles, xp_ref[...], xs_ref[...])
    h = _rms(x, gain_ref[...]).astype(BF16)
    qkv_w = qkv_ref.shape[1]
    for c in range(0, qkv_w, d):
        qkv_ref[:, c:c + d] = jnp.dot(h, wqkvz_ref[:, c:c + d], preferred_element_type=F32).astype(BF16)
    z_ref[...] = jnp.dot(h, wqkvz_ref[:, qkv_w:], preferred_element_type=F32).astype(BF16)
    ab_ref[...] = jnp.dot(h, wab_ref[...], preferred_element_type=F32)
    glu_a = jnp.dot(h, wglu_ref[:, :d], preferred_element_type=F32)
    glu_b = jnp.dot(h, wglu_ref[:, d:], preferred_element_type=F32)
    cv_ref[...] = glu_a * jax.nn.sigmoid(glu_b)
    for c in range(0, 2 * d, d):
        sg_ref[:, c:c + d] = jax.nn.sigmoid(
            jnp.dot(h, wgate_ref[:, c:c + d], preferred_element_type=F32)).astype(BF16)


def _inproj(xp, xs, gain, wqkvz, wab, wglu, wgate, qkv_w):
    n_p, d = xp.shape
    n_s = xs.shape[0]
    tm = TOKEN_TILE
    npt, nst = n_p // tm, n_s // tm
    n = n_p + n_s
    ab_w = wab.shape[1]
    row = lambda i: (i, 0)
    return pl.pallas_call(
        functools.partial(_inproj_body, n_prompt_tiles=npt),
        grid=(npt + nst,),
        in_specs=[
            pl.BlockSpec((tm, d), lambda i: (jnp.minimum(i, npt - 1), 0)),
            pl.BlockSpec((tm, d), lambda i: (jnp.maximum(i - npt, 0), 0)),
            _const_spec(gain.shape), _const_spec(wqkvz.shape), _const_spec(wab.shape),
            _const_spec(wglu.shape), _const_spec(wgate.shape),
        ],
        out_specs=[
            pl.BlockSpec((tm, qkv_w), row), pl.BlockSpec((tm, d), row), pl.BlockSpec((tm, ab_w), row),
            pl.BlockSpec((tm, d), row), pl.BlockSpec((tm, 2 * d), row),
        ],
        out_shape=[
            jax.ShapeDtypeStruct((n, qkv_w), BF16), jax.ShapeDtypeStruct((n, d), BF16),
            jax.ShapeDtypeStruct((n, ab_w), F32), jax.ShapeDtypeStruct((n, d), F32),
            jax.ShapeDtypeStruct((n, 2 * d), BF16),
        ],
        compiler_params=_params(1),
        name="inproj",
    )(xp, xs, gain, wqkvz, wab, wglu, wgate)


def _decay_scalars(ab, alog_row, dtb_row):
    return -jnp.exp(alog_row) * _softplus(ab + dtb_row), jax.nn.sigmoid(ab)


def _cumsum_rows(incl_bf16, g):
    hi, lo = _split_hi_lo(g)
    both = jnp.dot(incl_bf16, jnp.concatenate([hi, lo], axis=1), preferred_element_type=F32)
    return both[:, :LANES] + both[:, LANES:]


def _l2norm(x):
    return x * lax.rsqrt(jnp.sum(x * x, axis=-1, keepdims=True) + L2_EPS)


def _pair_masks(r, same):
    ri = lax.broadcasted_iota(I32, (r, r), 0)
    ci = lax.broadcasted_iota(I32, (r, r), 1)
    pair = same(ri, ci)
    as_f = lambda m: jnp.where(m, 1.0, 0.0).astype(F32)
    return as_f(ri == ci), as_f(pair & (ri >= ci)), -as_f(pair & (ri > ci))


def _lane_col(x, lane):
    return jnp.broadcast_to(x[:, lane:lane + 1], (x.shape[0], LANES))


def _wy_heads(q, k, v, g_all, beta_all, hps, eye, incl, neg_strict, n_doublings):
    r = q[0].shape[0]
    heads = range(len(q))
    gc_all = _cumsum_rows(incl.astype(BF16), g_all)
    pad = jnp.zeros((max(LANES - r, 0), LANES), F32)
    gc_t = jnp.transpose(jnp.concatenate([gc_all, pad], axis=0) if r < LANES else gc_all)
    gc = [_lane_col(gc_all, h) for h in heads]
    beta = [_lane_col(beta_all, hps + h) for h in heads]
    pair_decay = []
    for h in heads:
        gc_wide = jnp.concatenate([gc[h]] * (r // LANES), axis=1) if r > LANES else gc[h][:, :r]
        pair_decay.append(jnp.exp(jnp.minimum(gc_wide - gc_t[h:h + 1, :r], 0.0)))
    kb = [k[h] * beta[h] for h in heads]
    gram = [_mm_nt(jnp.concatenate([kb[h], q[h]], axis=0), k[h]) for h in heads]
    neg_m = [gram[h][:r] * (pair_decay[h] * neg_strict) for h in heads]
    p = [gram[h][r:] * (pair_decay[h] * incl) for h in heads]
    t_inv = [eye + neg_m[h] for h in heads]
    power = neg_m
    for _ in range(n_doublings):
        power = [_mm(power[h], power[h]) for h in heads]
        t_inv = [t_inv[h] + _mm(t_inv[h], power[h]) for h in heads]
    sol = [_mm(t_inv[h], jnp.concatenate([v[h] * beta[h], kb[h] * jnp.exp(gc[h])], axis=1)) for h in heads]
    return gc, [s[:, :LANES] for s in sol], [s[:, LANES:] for s in sol], p


def _gated_out(o, gnorm, z):
    on = o * lax.rsqrt(jnp.mean(o * o, axis=-1, keepdims=True) + RMS_EPS) * gnorm
    return (on * _silu(z)).astype(BF16)


def _gdn_prompt_body(q_ref, k_ref, v_ref, z_ref, ab_ref, cwq_ref, cwk_ref, cwv_ref, alog_ref, dtb_ref,
                     gnorm_ref, og_ref, sout_ref, tq_ref, tk_ref, tv_ref, halo_ref, s_ref, mask_ref, *, n_taps):
    hps, dk = s_ref.shape[0], s_ref.shape[1]
    grp, chunk = GDN_GROUP, GDN_CHUNK
    n_groups = q_ref.shape[0] // grp
    halo_rows = halo_ref.shape[1]
    halo_ref[...] = jnp.zeros(halo_ref.shape, F32)
    s_ref[...] = jnp.zeros(s_ref.shape, F32)
    for i, m in enumerate(_pair_masks(grp, lambda ri, ci: (ri // chunk) == (ci // chunk))):
        mask_ref[i] = m
    heads = range(hps)

    def group(gi, carry):
        rows = pl.ds(pl.multiple_of(gi * grp, grp), grp)
        conv = []
        for part, (x_ref, cw_ref) in enumerate(((q_ref, cwq_ref), (k_ref, cwk_ref), (v_ref, cwv_ref))):
            xg = x_ref[rows, :].astype(F32)
            xw = jnp.concatenate([halo_ref[part], xg], axis=0)
            halo_ref[part] = xg[grp - halo_rows:, :]
            y = None
            for j in range(n_taps):
                off = halo_rows - (n_taps - 1) + j
                term = cw_ref[j:j + 1, :] * xw[off:off + grp, :]
                y = term if y is None else y + term
            conv.append(_silu(y))
        sl = [slice(h * dk, (h + 1) * dk) for h in heads]
        q = [_l2norm(conv[0][:, sl[h]]) * (dk ** -0.5) for h in heads]
        k = [_l2norm(conv[1][:, sl[h]]) for h in heads]
        v = [conv[2][:, sl[h]] for h in heads]
        g_all, beta_all = _decay_scalars(ab_ref[rows, :], alog_ref[0:1, :], dtb_ref[0:1, :])
        gc, u, w, p = _wy_heads(q, k, v, g_all, beta_all, hps, mask_ref[0], mask_ref[1], mask_ref[2], 5)
        qe = [q[h] * jnp.exp(gc[h]) for h in heads]
        state = [s_ref[h] for h in heads]
        outs = [[] for _ in heads]
        for j in range(grp // chunk):
            cs = slice(j * chunk, (j + 1) * chunk)
            g_last = [gc[h][(j + 1) * chunk - 1:(j + 1) * chunk, :] for h in heads]
            k_dec = [k[h][cs] * jnp.exp(g_last[h] - gc[h][cs]) for h in heads]
            ws_qs = [_mm(jnp.concatenate([w[h][cs], qe[h][cs]], axis=0), state[h]) for h in heads]
            v_new = [u[h][cs] - ws_qs[h][:chunk] for h in heads]
            for h in heads:
                outs[h].append(ws_qs[h][chunk:] + _mm(p[h][cs, cs], v_new[h]))
            state = [state[h] * jnp.exp(g_last[h]) + lax.dot_general(
                k_dec[h].astype(BF16), v_new[h].astype(BF16), (((0,), (0,)), ((), ())),
                preferred_element_type=F32) for h in heads]
        for h in heads:
            s_ref[h] = state[h]
            o = jnp.concatenate(outs[h], axis=0)
            og_ref[rows, sl[h]] = _gated_out(o, gnorm_ref[...], z_ref[rows, sl[h]].astype(F32))
        return carry

    lax.fori_loop(0, n_groups, group, 0)
    sout_ref[0] = s_ref[...]
    tq_ref[0] = halo_ref[0]
    tk_ref[0] = halo_ref[1]
    tv_ref[0] = halo_ref[2]


def _gdn_prompt(qkv, z, ab, conv_w, alog, dtb, gnorm, n_batch, seq, n_heads, dk):
    hps = GDN_HEADS_PER_STEP
    hgw = hps * dk
    n_hg = n_heads // hps
    n_taps = conv_w.shape[1]
    grp = GDN_GROUP
    col = lambda part: (lambda b, h: (b, part * n_hg + h))
    cwcol = lambda part: (lambda b, h: (0, 0, part * n_hg + h))
    bh = lambda b, h: (b, h)
    tail = jax.ShapeDtypeStruct((n_batch, SUBLANES, n_heads * dk), F32)
    tail_spec = pl.BlockSpec((1, SUBLANES, hgw), lambda b, h: (b, 0, h))
    return pl.pallas_call(
        functools.partial(_gdn_prompt_body, n_taps=n_taps),
        grid=(n_batch, n_hg),
        in_specs=[
            pl.BlockSpec((seq, hgw), col(0)), pl.BlockSpec((seq, hgw), col(1)), pl.BlockSpec((seq, hgw), col(2)),
            pl.BlockSpec((seq, hgw), bh),
            pl.BlockSpec((seq, LANES), bh),
            pl.BlockSpec((None, n_taps, hgw), cwcol(0)), pl.BlockSpec((None, n_taps, hgw), cwcol(1)),
            pl.BlockSpec((None, n_taps, hgw), cwcol(2)),
            pl.BlockSpec((SUBLANES, LANES), lambda b, h: (h, 0)),
            pl.BlockSpec((SUBLANES, LANES), lambda b, h: (h, 0)),
            _const_spec(gnorm.shape),
        ],
        out_specs=[
            pl.BlockSpec((seq, hgw), bh),
            pl.BlockSpec((1, hps, dk, dk), lambda b, h: (b, h, 0, 0)),
            tail_spec, tail_spec, tail_spec,
        ],
        out_shape=[
            jax.ShapeDtypeStruct((n_batch * seq, n_heads * dk), BF16),
            jax.ShapeDtypeStruct((n_batch, n_heads, dk, dk), F32),
            tail, tail, tail,
        ],
        scratch_shapes=[pltpu.VMEM((3, SUBLANES, hgw), F32), pltpu.VMEM((hps, dk, dk), F32),
                        pltpu.VMEM((3, grp, grp), F32)],
        compiler_params=_params(2),
        name="gdn_prompt",
    )(qkv, qkv, qkv, z, ab, conv_w, conv_w, conv_w, alog, dtb, gnorm)


def _gdn_sample_body(*refs, n_tok, n_taps):
    sb = SEQ_BLOCK
    it = iter(refs)
    take = lambda n: [next(it) for _ in range(n)]
    x_refs = [take(n_tok) for _ in range(3)]
    cb_refs = take(3)
    z_refs = take(n_tok)
    ab_refs = take(n_tok)
    s0_ref = next(it)
    cw_refs = take(3)
    alog_ref, dtb_ref, gnorm_ref = take(3)
    og_refs = take(n_tok)
    sout_ref = next(it)

    hps, dk = s0_ref.shape[1], s0_ref.shape[2]
    heads = range(hps)
    r = n_tok * sb

    conv = []
    for part in range(3):
        ext = [cb_refs[part][:, j, :] for j in range(n_taps - 1)]
        ext += [x_refs[part][t][...].astype(F32) for t in range(n_tok)]
        ys = []
        for t in range(n_tok):
            y = None
            for j in range(n_taps):
                term = cw_refs[part][j:j + 1, :] * ext[t + j]
                y = term if y is None else y + term
            ys.append(_silu(y))
        conv.append(jnp.concatenate(ys, axis=0))
    ab = jnp.concatenate([a[...] for a in ab_refs], axis=0)

    eye, incl, neg_strict = _pair_masks(r, lambda ri, ci: (ri % sb) == (ci % sb))
    row_seq = lax.broadcasted_iota(I32, (2 * r, dk), 0) % sb
    col_seq = lax.broadcasted_iota(I32, (dk, LANES), 1) % sb
    n_doublings = max((n_tok - 1).bit_length() - 1, 0)

    sl = [slice(h * dk, (h + 1) * dk) for h in heads]
    q = [_l2norm(conv[0][:, sl[h]]) * (dk ** -0.5) for h in heads]
    k = [_l2norm(conv[1][:, sl[h]]) for h in heads]
    v = [conv[2][:, sl[h]] for h in heads]
    g_all, beta_all = _decay_scalars(ab, alog_ref[0:1, :], dtb_ref[0:1, :])
    gc, u, w, p = _wy_heads(q, k, v, g_all, beta_all, hps, eye, incl, neg_strict, n_doublings)
    for h in heads:
        wq = jnp.concatenate([w[h], q[h] * jnp.exp(gc[h])], axis=0).astype(BF16)
        ws_qs = jnp.zeros((2 * r, dk), F32)
        for s in range(sb):
            ws_qs = jnp.where(row_seq == s, _mm(wq, s0_ref[s, h]), ws_qs)
        v_new = u[h] - ws_qs[:r]
        o = ws_qs[r:] + _mm(p[h], v_new)
        z = jnp.concatenate([zr[:, sl[h]] for zr in z_refs], axis=0).astype(F32)
        og = _gated_out(o, gnorm_ref[...], z)
        for t in range(n_tok):
            og_refs[t][:, sl[h]] = og[t * sb:(t + 1) * sb]
        g_end = gc[h][r - sb:, :]
        k_dec = k[h] * jnp.exp(jnp.concatenate([g_end] * n_tok, axis=0) - gc[h])
        k_dec_t = jnp.transpose(jnp.concatenate([k_dec, jnp.zeros((LANES - r, dk), F32)], axis=0))
        v_new_pad = jnp.concatenate([v_new, jnp.zeros((LANES - r, dk), F32)], axis=0).astype(BF16)
        for s in range(sb):
            upd = _mm(jnp.where(col_seq == s, k_dec_t, 0.0), v_new_pad)
            sout_ref[s, h] = s0_ref[s, h] * jnp.exp(g_end[s:s + 1, :]) + upd


def _gdn_sample(qkv, z, ab, conv_buf, s0, conv_w, alog, dtb, gnorm, n_prompt_rows, n_seq, n_tok, n_heads, dk):
    sb, hps = SEQ_BLOCK, GDN_HEADS_PER_STEP
    hgw = hps * dk
    n_hg = n_heads // hps
    n_taps = conv_w.shape[1]
    base = n_prompt_rows // sb
    per_t = n_seq // sb

    def tok_rows(t, colfn):
        return lambda i, h: (base + t * per_t + i, colfn(h))

    in_specs, args = [], []
    for part in range(3):
        for t in range(n_tok):
            in_specs.append(pl.BlockSpec((sb, hgw), tok_rows(t, lambda h, part=part: part * n_hg + h)))
            args.append(qkv)
    for part in range(3):
        in_specs.append(pl.BlockSpec((None, sb, n_taps - 1, hgw),
                                     lambda i, h, part=part: (0, i, 0, part * n_hg + h)))
        args.append(conv_buf)
    for t in range(n_tok):
        in_specs.append(pl.BlockSpec((sb, hgw), tok_rows(t, lambda h: h)))
        args.append(z)
    for t in range(n_tok):
        in_specs.append(pl.BlockSpec((sb, LANES), tok_rows(t, lambda h: h)))
        args.append(ab)
    in_specs.append(pl.BlockSpec((None, sb, hps, dk, dk), lambda i, h: (0, i, h, 0, 0)))
    args.append(s0)
    for part in range(3):
        in_specs.append(pl.BlockSpec((None, n_taps, hgw), lambda i, h, part=part: (0, 0, part * n_hg + h)))
        args.append(conv_w)
    in_specs += [pl.BlockSpec((SUBLANES, LANES), lambda i, h: (h, 0)),
                 pl.BlockSpec((SUBLANES, LANES), lambda i, h: (h, 0)), _const_spec(gnorm.shape)]
    args += [alog, dtb, gnorm]
    outs = pl.pallas_call(
        functools.partial(_gdn_sample_body, n_tok=n_tok, n_taps=n_taps),
        grid=(per_t, n_hg),
        in_specs=in_specs,
        out_specs=[pl.BlockSpec((sb, hgw), lambda i, h: (i, h))] * n_tok
        + [pl.BlockSpec((sb, hps, dk, dk), lambda i, h: (i, h, 0, 0))],
        out_shape=[jax.ShapeDtypeStruct((n_seq, n_heads * dk), BF16)] * n_tok
        + [jax.ShapeDtypeStruct(s0.shape[1:], F32)],
        compiler_params=_params(2),
        name="gdn_sample",
    )(*args)
    return jnp.concatenate(outs[:n_tok], axis=0), outs[n_tok]


def _ln_silu(y, lg, lb):
    mu = jnp.mean(y, axis=-1, keepdims=True)
    yc = y - mu
    var = jnp.mean(yc * yc, axis=-1, keepdims=True)
    return _silu(yc * lax.rsqrt(var + LN_EPS) * lg + lb).astype(BF16)


def _conf_prompt_body(cv_ref, halo_ref, cw_ref, cb_ref, lg_ref, lb_ref, out_ref, tail_ref, rot_ref, conv_ref,
                      *, n_taps):
    tile_in_seq = pl.program_id(1)
    tt, c = cv_ref.shape
    halo = CONF_HALO
    rot_ref[0, 0:halo, :] = jnp.where(tile_in_seq == 0, 0.0, halo_ref[...])
    rot_ref[0, halo:halo + tt, :] = cv_ref[...]
    rot_ref[0, halo + tt:halo + tt + SUBLANES, :] = jnp.zeros((SUBLANES, c), F32)
    step = 32

    def rotate(i, carry):
        base = pl.multiple_of(i * step, step)
        win = rot_ref[0, pl.ds(base, step + SUBLANES), :]
        for r in range(1, SUBLANES):
            rot_ref[r, pl.ds(base, step), :] = win[r:r + step, :]
        return carry

    lax.fori_loop(0, (halo + tt) // step, rotate, 0)
    lead = halo - (n_taps - 1)
    rows = CONF_ROWS
    groups = rows // SUBLANES
    lane_w = c // CONF_LANE_SPLIT
    by_rot = [[(j, (lead + j) // SUBLANES) for j in range(n_taps) if (lead + j) % SUBLANES == r]
              for r in range(SUBLANES)]

    for lh in range(CONF_LANE_SPLIT):
        ls = slice(lh * lane_w, (lh + 1) * lane_w)

        def conv(i, carry, ls=ls):
            base = pl.multiple_of(i * rows, rows)
            accs = [jnp.broadcast_to(cb_ref[:, ls], (SUBLANES, lane_w))] * groups
            for r, taps in enumerate(by_rot):
                if not taps:
                    continue
                lo = min(a for _, a in taps)
                hi = max(a for _, a in taps) + groups
                slabs = {m: rot_ref[r, pl.ds(pl.multiple_of(base + m * SUBLANES, SUBLANES), SUBLANES), ls]
                         for m in range(lo, hi)}
                for j, a in taps:
                    w8 = cw_ref[j, :, ls]
                    accs = [acc + w8 * slabs[a + g] for g, acc in enumerate(accs)]
            conv_ref[pl.ds(base, rows), ls] = jnp.concatenate(accs, axis=0)
            return carry

        lax.fori_loop(0, tt // rows, conv, 0)

    nrows = CONF_NORM_ROWS

    def norm(i, carry):
        base = pl.multiple_of(i * nrows, nrows)
        out_ref[pl.ds(base, nrows), :] = _ln_silu(conv_ref[pl.ds(base, nrows), :], lg_ref[...], lb_ref[...])
        return carry

    lax.fori_loop(0, tt // nrows, norm, 0)

    @pl.when(tile_in_seq == pl.num_programs(1) - 1)
    def _():
        tail_ref[0] = cv_ref[tt - halo:, :]


def _conf_prompt(cv, cw, cb, lg, lb, n_batch, seq):
    c = cv.shape[1]
    tt = TOKEN_TILE
    tiles = seq // tt
    halo_per_tile = tt // CONF_HALO
    n_taps = cw.shape[1]
    cw8 = jnp.broadcast_to(cw.reshape(n_taps, 1, c), (n_taps, SUBLANES, c))
    return pl.pallas_call(
        functools.partial(_conf_prompt_body, n_taps=n_taps),
        grid=(n_batch, tiles),
        in_specs=[
            pl.BlockSpec((tt, c), lambda b, j: (b * tiles + j, 0)),
            pl.BlockSpec((CONF_HALO, c), lambda b, j: (jnp.maximum((b * tiles + j) * halo_per_tile - 1, 0), 0)),
            _const_spec(cw8.shape), _const_spec(cb.shape), _const_spec(lg.shape), _const_spec(lb.shape),
        ],
        out_specs=[pl.BlockSpec((tt, c), lambda b, j: (b * tiles + j, 0)),
                   pl.BlockSpec((1, CONF_HALO, c), lambda b, j: (b, 0, 0))],
        out_shape=[jax.ShapeDtypeStruct((n_batch * seq, c), BF16),
                   jax.ShapeDtypeStruct((n_batch, CONF_HALO, c), F32)],
        scratch_shapes=[pltpu.VMEM((SUBLANES, CONF_HALO + tt + SUBLANES, c), F32), pltpu.VMEM((tt, c), F32)],
        compiler_params=_params(2),
        name="conf_prompt",
    )(cv, cv, cw8, cb, lg, lb)


def _conf_sample_body(*refs, n_tok, n_taps):
    buf_ref = refs[0]
    cv_refs = refs[1:1 + n_tok]
    cw_ref, cb_ref, lg_ref, lb_ref = refs[1 + n_tok:5 + n_tok]
    out_refs = refs[5 + n_tok:5 + 2 * n_tok]
    newbuf_ref = refs[5 + 2 * n_tok]
    n_hist = n_taps - 1

    def ext(i):
        return buf_ref[:, i, :] if i < n_hist else cv_refs[i - n_hist][...]

    for t in range(n_tok):
        acc = jnp.broadcast_to(cb_ref[...], cv_refs[0].shape)
        for j in range(n_taps):
            acc = acc + cw_ref[j:j + 1, :] * ext(t + j)
        out_refs[t][...] = _ln_silu(acc, lg_ref[...], lb_ref[...])
    for i in range(n_hist):
        newbuf_ref[:, i, :] = ext(i + n_tok)


def _conf_sample(cv, buf, cw, cb, lg, lb, n_prompt_rows, n_seq, n_tok):
    sb = SEQ_BLOCK
    c = cv.shape[1]
    n_taps = cw.shape[1]
    base, per_t = n_prompt_rows // sb, n_seq // sb
    in_specs = [pl.BlockSpec((None, sb, n_taps - 1, c), lambda i: (0, i, 0, 0))]
    in_specs += [pl.BlockSpec((sb, c), lambda i, t=t: (base + t * per_t + i, 0)) for t in range(n_tok)]
    in_specs += [_layer_spec(cw.shape), _const_spec(cb.shape), _const_spec(lg.shape), _const_spec(lb.shape)]
    outs = pl.pallas_call(
        functools.partial(_conf_sample_body, n_tok=n_tok, n_taps=n_taps),
        grid=(per_t,),
        in_specs=in_specs,
        out_specs=[pl.BlockSpec((sb, c), lambda i: (i, 0))] * n_tok
        + [pl.BlockSpec((sb, n_taps - 1, c), lambda i: (i, 0, 0))],
        out_shape=[jax.ShapeDtypeStruct((n_seq, c), BF16)] * n_tok
        + [jax.ShapeDtypeStruct((n_seq, n_taps - 1, c), F32)],
        compiler_params=_params(1),
        name="conf_sample",
    )(buf, *([cv] * n_tok), cw, cb, lg, lb)
    return jnp.concatenate(outs[:n_tok], axis=0), outs[n_tok]


def _merge_body(ogp_ref, ogs_ref, ybp_ref, ybs_ref, sg_ref, xp_ref, xs_ref, wgo_ref, wpw_ref, bpw_ref,
                wout_ref, gxa_ref, wq_ref, x1_ref, q_ref, *, n_prompt_tiles):
    is_prompt = pl.program_id(0) < n_prompt_tiles
    d = xp_ref.shape[1]
    og = jnp.where(is_prompt, ogp_ref[...], ogs_ref[...])
    yb_in = jnp.where(is_prompt, ybp_ref[...], ybs_ref[...])
    x = jnp.where(is_prompt, xp_ref[...], xs_ref[...])
    y_a = jnp.dot(og, wgo_ref[...], preferred_element_type=F32)
    y_b = jnp.dot(yb_in, wpw_ref[...], preferred_element_type=F32) + bpw_ref[...]
    mixed = sg_ref[:, :d].astype(F32) * y_a + sg_ref[:, d:].astype(F32) * y_b
    x1 = x + jnp.dot(mixed.astype(BF16), wout_ref[...], preferred_element_type=F32)
    x1_ref[...] = x1
    q_ref[...] = jnp.dot(_rms(x1, gxa_ref[...]).astype(BF16), wq_ref[...],
                         preferred_element_type=F32).astype(BF16)


def _dual(tm, cols, npt):
    return [pl.BlockSpec((tm, cols), lambda i: (jnp.minimum(i, npt - 1), 0)),
            pl.BlockSpec((tm, cols), lambda i: (jnp.maximum(i - npt, 0), 0))]


def _merge(og_p, og_s, yb_p, yb_s, sg, xp, xs, wgo, wpw, bpw, wout, gxa, wq):
    n_p, d = xp.shape
    tm = TOKEN_TILE
    npt, nst = n_p // tm, xs.shape[0] // tm
    n = n_p + xs.shape[0]
    row = lambda i: (i, 0)
    return pl.pallas_call(
        functools.partial(_merge_body, n_prompt_tiles=npt),
        grid=(npt + nst,),
        in_specs=_dual(tm, d, npt) + _dual(tm, d, npt) + [pl.BlockSpec((tm, 2 * d), row)] + _dual(tm, d, npt)
        + [_const_spec(a.shape) for a in (wgo, wpw, bpw, wout, gxa, wq)],
        out_specs=[pl.BlockSpec((tm, d), row), pl.BlockSpec((tm, d), row)],
        out_shape=[jax.ShapeDtypeStruct((n, d), F32), jax.ShapeDtypeStruct((n, d), BF16)],
        compiler_params=_params(1),
        name="merge",
    )(og_p, og_s, yb_p, yb_s, sg, xp, xs, wgo, wpw, bpw, wout, gxa, wq)


def _memkv_body(m_ref, gain_ref, w_ref, k_ref, v_ref):
    d = k_ref.shape[1]
    h = _rms(m_ref[...], gain_ref[...]).astype(BF16)
    k_ref[...] = jnp.dot(h, w_ref[:, :d], preferred_element_type=F32)
    v_ref[...] = jnp.dot(h, w_ref[:, d:], preferred_element_type=F32)


def _memkv(mem, gain, w):
    n, d = mem.shape
    tm = min(TOKEN_TILE, n)
    row = lambda i: (i, 0)
    return pl.pallas_call(
        _memkv_body,
        grid=(n // tm,),
        in_specs=[pl.BlockSpec((tm, d), row), _const_spec(gain.shape), _const_spec(w.shape)],
        out_specs=[pl.BlockSpec((tm, d), row)] * 2,
        out_shape=[jax.ShapeDtypeStruct((n, d), F32)] * 2,
        compiler_params=_params(1),
        name="memkv",
    )(mem, gain, w)


def _softmax_rows(s):
    e = jnp.exp(s - jnp.max(s, axis=-1, keepdims=True))
    return e / jnp.sum(e, axis=-1, keepdims=True)


def _attn_prompt_body(q_ref, k_ref, v_ref, o_ref, *, n_heads):
    dh = q_ref.shape[1] // n_heads
    for h in range(n_heads):
        sl = slice(h * dh, (h + 1) * dh)
        s = _mm_nt(q_ref[:, sl], k_ref[:, sl]) * (dh ** -0.5)
        o_ref[:, sl] = _mm(_softmax_rows(s), v_ref[:, sl]).astype(BF16)


def _attn_prompt(q, k, v, n_batch, seq, mem_len, n_heads):
    d = q.shape[1]
    tq = TOKEN_TILE
    tiles = seq // tq
    return pl.pallas_call(
        functools.partial(_attn_prompt_body, n_heads=n_heads),
        grid=(n_batch, tiles),
        in_specs=[
            pl.BlockSpec((tq, d), lambda b, j: (b * tiles + j, 0)),
            pl.BlockSpec((mem_len, d), lambda b, j: (b, 0)),
            pl.BlockSpec((mem_len, d), lambda b, j: (b, 0)),
        ],
        out_specs=pl.BlockSpec((tq, d), lambda b, j: (b * tiles + j, 0)),
        out_shape=jax.ShapeDtypeStruct((n_batch * seq, d), BF16),
        compiler_params=_params(2),
        name="attn_prompt",
    )(q, k, v)


def _attn_sample_body(q_ref, k_hbm, v_hbm, o_ref, kbuf, vbuf, sem):
    _, sb, mem_len, dh = kbuf.shape
    n_heads = k_hbm.shape[3]
    rows_per_seq = q_ref.shape[0] // sb
    blk = pl.program_id(0)

    def head_copies(block, h):
        seqs = pl.ds(block * sb, sb)
        return (pltpu.make_async_copy(k_hbm.at[0, seqs, :, h, :], kbuf.at[h % 2], sem.at[0, h % 2]),
                pltpu.make_async_copy(v_hbm.at[0, seqs, :, h, :], vbuf.at[h % 2], sem.at[1, h % 2]))

    def start(block, h):
        for cp in head_copies(block, h):
            cp.start()

    @pl.when(blk == 0)
    def _():
        start(blk, 0)

    for h in range(n_heads):
        if h + 1 < n_heads:
            start(blk, h + 1)
        else:
            @pl.when(blk + 1 < pl.num_programs(0))
            def _():
                start(blk + 1, 0)
        for cp in head_copies(blk, h):
            cp.wait()
        sl = slice(h * dh, (h + 1) * dh)
        keys = kbuf[h % 2].reshape(sb * mem_len, dh)
        vals = vbuf[h % 2].reshape(sb * mem_len, dh)
        s = _mm_nt(q_ref[:, sl], keys) * (dh ** -0.5)
        row_seq = lax.broadcasted_iota(I32, s.shape, 0) // rows_per_seq
        col_seq = lax.broadcasted_iota(I32, s.shape, 1) // mem_len
        p = _softmax_rows(jnp.where(row_seq == col_seq, s, -jnp.inf))
        o_ref[:, sl] = _mm(p, vals).astype(BF16)


def _attn_sample(q_seq_major, kc, vc, n_tok):
    n_s, d = q_seq_major.shape
    _, _, mem_len, n_heads, dh = kc.shape
    assert n_heads % 2 == 0
    sb = SEQ_BLOCK
    rows = sb * n_tok
    return pl.pallas_call(
        _attn_sample_body,
        grid=(n_s // rows,),
        in_specs=[pl.BlockSpec((rows, d), lambda i: (i, 0)), pl.BlockSpec(memory_space=pl.ANY),
                  pl.BlockSpec(memory_space=pl.ANY)],
        out_specs=pl.BlockSpec((rows, d), lambda i: (i, 0)),
        out_shape=jax.ShapeDtypeStruct((n_s, d), BF16),
        scratch_shapes=[pltpu.VMEM((2, sb, mem_len, dh), F32), pltpu.VMEM((2, sb, mem_len, dh), F32),
                        pltpu.SemaphoreType.DMA((2, 2))],
        compiler_params=_params(1),
        name="attn_sample",
    )(q_seq_major, kc, vc)


def _router_body(op_ref, os_ref, x1_ref, wo_ref, gain_ref, rwt_ref, rb_ref,
                 x2_ref, hf_ref, idx_ref, gate_ref, *, n_prompt_tiles):
    o = jnp.where(pl.program_id(0) < n_prompt_tiles, op_ref[...], os_ref[...])
    x2 = x1_ref[...] + jnp.dot(o, wo_ref[...], preferred_element_type=F32)
    x2_ref[...] = x2
    hf = _rms(x2, gain_ref[...])
    hf_ref[...] = hf
    logits = lax.dot_general(rwt_ref[...], hf, (((1,), (1,)), ((), ())), precision=HIGHEST,
                             preferred_element_type=F32) + rb_ref[...]
    n_exp, tm = logits.shape
    expert = lax.broadcasted_iota(I32, logits.shape, 0)
    vals, picks = [], []
    for _ in range(TOP_K):
        best = jnp.max(logits, axis=0, keepdims=True)
        pick = jnp.min(jnp.where(logits == best, expert, n_exp), axis=0, keepdims=True)
        vals.append(best)
        picks.append(pick)
        logits = jnp.where(expert == pick, -jnp.inf, logits)
    idx_ref[...] = jnp.concatenate(picks, axis=0)
    es = [jnp.exp(v - vals[0]) for v in vals]
    total = functools.reduce(lambda a, b: a + b, es)
    gates = jnp.concatenate([e / total for e in es] + [jnp.zeros((LANES - TOP_K, tm), F32)], axis=0)
    gate_ref[...] = jnp.transpose(gates)


def _router(o_p, o_s, x1, wo, gain, rwt, rb):
    n, d = x1.shape
    tm = TOKEN_TILE
    npt = o_p.shape[0] // tm
    row = lambda i: (i, 0)
    return pl.pallas_call(
        functools.partial(_router_body, n_prompt_tiles=npt),
        grid=(n // tm,),
        in_specs=_dual(tm, d, npt) + [pl.BlockSpec((tm, d), row)]
        + [_const_spec(a.shape) for a in (wo, gain, rwt, rb)],
        out_specs=[pl.BlockSpec((tm, d), row), pl.BlockSpec((tm, d), row),
                   pl.BlockSpec((TOP_K, tm), lambda i: (0, i)), pl.BlockSpec((tm, LANES), row)],
        out_shape=[jax.ShapeDtypeStruct((n, d), F32), jax.ShapeDtypeStruct((n, d), F32),
                   jax.ShapeDtypeStruct((TOP_K, n), I32), jax.ShapeDtypeStruct((n, LANES), F32)],
        compiler_params=_params(1),
        name="router",
    )(o_p, o_s, x1, wo, gain, rwt, rb)


def _positions_body(idx_ref, pos_ref, counts_ref, rank_ref, *, n_experts):
    ch = CUMSUM_CHUNK
    n = idx_ref.shape[1]
    n_chunks = n // ch
    upper = (lax.broadcasted_iota(I32, (ch, ch), 0) <= lax.broadcasted_iota(I32, (ch, ch), 1)).astype(BF16)
    expert = lax.broadcasted_iota(I32, (n_experts, ch), 0)

    def onehot(k, c):
        cols = pl.ds(pl.multiple_of(c * ch, ch), ch)
        return cols, expert == idx_ref[k:k + 1, cols]

    running = jnp.zeros((n_experts, 1), F32)
    for k in range(TOP_K):
        def count(c, run, k=k):
            cols, oh = onehot(k, c)
            cum = jnp.dot(oh.astype(BF16), upper, preferred_element_type=F32)
            rank_ref[k:k + 1, cols] = jnp.sum(jnp.where(oh, cum - 1.0 + run, 0.0), axis=0, keepdims=True)
            return run + cum[:, ch - 1:ch]
        running = lax.fori_loop(0, n_chunks, count, running)

    counts = jnp.broadcast_to(running, (n_experts, LANES))
    counts_ref[...] = counts.astype(I32)
    lower = (lax.broadcasted_iota(I32, (n_experts, n_experts), 0)
             > lax.broadcasted_iota(I32, (n_experts, n_experts), 1)).astype(F32)
    start = jnp.dot(lower, counts, precision=HIGHEST, preferred_element_type=F32)[:, 0:1]

    for k in range(TOP_K):
        def place(c, carry, k=k):
            cols, oh = onehot(k, c)
            first = jnp.sum(jnp.where(oh, start, 0.0), axis=0, keepdims=True)
            pos_ref[k:k + 1, cols] = (rank_ref[k:k + 1, cols] + first).astype(I32)
            return carry
        lax.fori_loop(0, n_chunks, place, 0)


def _positions(idx, n_experts):
    k, n = idx.shape
    return pl.pallas_call(
        functools.partial(_positions_body, n_experts=n_experts),
        out_shape=[jax.ShapeDtypeStruct((k, n), I32), jax.ShapeDtypeStruct((n_experts, LANES), I32)],
        scratch_shapes=[pltpu.VMEM((k, n), F32)],
        compiler_params=pltpu.CompilerParams(vmem_limit_bytes=V7X_VMEM_REQUEST),
        name="positions",
    )(idx)


def _row_copy(src_ref, src_row, dst_ref, dst_row, sem):
    return pltpu.make_async_copy(src_ref.at[pl.ds(src_row, 1), :], dst_ref.at[pl.ds(dst_row, 1), :], sem)


def _dispatch_body(pos_ref, hf_ref, xs_ref, sem):
    tm = hf_ref.shape[0]

    def issue(t, carry):
        for k in range(TOP_K):
            _row_copy(hf_ref, t, xs_ref, pos_ref[k, t], sem).start()
        return carry

    lax.fori_loop(0, tm, issue, 0)

    def drain(t, carry):
        for k in range(TOP_K):
            _row_copy(hf_ref, t, xs_ref, pos_ref[k, t], sem).wait()
        return carry

    lax.fori_loop(0, tm, drain, 0)


def _dispatch(pos, hf):
    n, d = hf.shape
    tm = TOKEN_TILE
    return pl.pallas_call(
        _dispatch_body,
        grid=(n // tm,),
        in_specs=[pl.BlockSpec((TOP_K, tm), lambda i: (0, i), memory_space=pltpu.SMEM),
                  pl.BlockSpec((tm, d), lambda i: (i, 0))],
        out_specs=pl.BlockSpec(memory_space=pl.ANY),
        out_shape=jax.ShapeDtypeStruct((TOP_K * n, d), F32),
        scratch_shapes=[pltpu.SemaphoreType.DMA(())],
        compiler_params=_params(1),
        name="dispatch",
    )(pos, hf)


def _gmm_body(tile_ref, exp_ref, lo_ref, hi_ref, first_ref, slot_ref, next_ref, nitems_ref,
              xs_ref, w1_hbm, b1_ref, w2_hbm, b2_ref, ys_ref, w1f_ref, w2f_ref, w1b_ref, w2b_ref, sem):
    w = pl.program_id(0)

    def weight_copies(expert, slot):
        return (pltpu.make_async_copy(w1_hbm.at[0, expert], w1f_ref.at[slot], sem.at[0, slot]),
                pltpu.make_async_copy(w2_hbm.at[0, expert], w2f_ref.at[slot], sem.at[1, slot]))

    @pl.when(w < nitems_ref[0])
    def _():
        slot = slot_ref[w]
        new_expert = jnp.logical_or(w == 0, exp_ref[w] != exp_ref[jnp.maximum(w - 1, 0)])

        @pl.when(w == 0)
        def _():
            for cp in weight_copies(exp_ref[0], slot):
                cp.start()

        @pl.when(new_expert)
        def _():
            @pl.when(next_ref[w] >= 0)
            def _():
                for cp in weight_copies(next_ref[w], 1 - slot):
                    cp.start()

            for cp in weight_copies(exp_ref[w], slot):
                cp.wait()
            w1b_ref[...] = w1f_ref[slot].astype(BF16)
            w2b_ref[...] = w2f_ref[slot].astype(BF16)

        d_ff = w2b_ref.shape[0]
        hid = jnp.dot(xs_ref[...].astype(BF16), w1b_ref[...], preferred_element_type=F32) + b1_ref[0]
        glu = jnp.minimum(hid[:, :d_ff], SWIGLU_LIMIT)
        lin = jnp.clip(hid[:, d_ff:], -SWIGLU_LIMIT, SWIGLU_LIMIT)
        act = glu * jax.nn.sigmoid(SWIGLU_ALPHA * glu) * (lin + 1.0)
        y = jnp.dot(act.astype(BF16), w2b_ref[...], preferred_element_type=F32) + b2_ref[0]
        rows = lax.broadcasted_iota(I32, (ys_ref.shape[0], 1), 0)
        y = jnp.where((rows >= lo_ref[w]) & (rows < hi_ref[w]), y, 0.0)

        @pl.when(first_ref[w] == 1)
        def _():
            ys_ref[...] = y

        @pl.when(first_ref[w] == 0)
        def _():
            ys_ref[...] += y


def _group_work_items(counts, n_items):
    tr = MOE_ROW_TILE
    n_exp = counts.shape[0]
    experts = jnp.arange(n_exp, dtype=I32)
    ends = jnp.cumsum(counts)
    starts = ends - counts
    first_tile = starts // tr
    last_tile = jnp.maximum(ends - 1, 0) // tr
    per_expert = jnp.where(counts > 0, last_tile - first_tile + 1, 0)
    item_end = jnp.cumsum(per_expert)
    total = item_end[-1]
    w = jnp.minimum(jnp.arange(n_items, dtype=I32), total - 1)
    e = jnp.sum((item_end[None, :] <= w[:, None]).astype(I32), axis=1)
    onehot = (e[:, None] == experts[None, :]).astype(I32)
    pick = lambda a: jnp.sum(onehot * a[None, :], axis=1)
    tile = pick(first_tile) + (w - pick(item_end - per_expert))
    valid = jnp.arange(n_items, dtype=I32) < total
    lo = jnp.where(valid, jnp.maximum(pick(starts) - tile * tr, 0), 0)
    hi = jnp.where(valid, jnp.minimum(pick(ends) - tile * tr, tr), 0)
    first = jnp.concatenate([jnp.ones((1,), I32), (tile[1:] != tile[:-1]).astype(I32)])
    later = (experts[None, :] > experts[:, None]) & (per_expert[None, :] > 0)
    next_expert = jnp.min(jnp.where(later, experts[None, :], n_exp), axis=1)
    next_expert = jnp.where(next_expert < n_exp, next_expert, -1)
    rank = jnp.cumsum((per_expert > 0).astype(I32)) - 1
    as_i32 = lambda a: a.astype(I32)
    return (as_i32(tile), as_i32(e), as_i32(lo), as_i32(hi), as_i32(first), as_i32(pick(rank) % 2),
            as_i32(pick(next_expert)), as_i32(total).reshape(1))


def _gmm(xs, counts, w1, b1, w2, b2):
    n_rows, d = xs.shape
    _, n_exp, _, two_ff = w1.shape
    d_ff = two_ff // 2
    tr = MOE_ROW_TILE
    n_items = n_rows // tr + n_exp - 1
    items = _group_work_items(counts, n_items)
    grid_spec = pltpu.PrefetchScalarGridSpec(
        num_scalar_prefetch=len(items),
        grid=(n_items,),
        in_specs=[
            pl.BlockSpec((tr, d), lambda w, tile, *_: (tile[w], 0)),
            pl.BlockSpec(memory_space=pl.ANY),
            pl.BlockSpec((1, 1, two_ff), lambda w, tile, exp, *_: (exp[w], 0, 0)),
            pl.BlockSpec(memory_space=pl.ANY),
            pl.BlockSpec((1, 1, d), lambda w, tile, exp, *_: (exp[w], 0, 0)),
        ],
        out_specs=pl.BlockSpec((tr, d), lambda w, tile, *_: (tile[w], 0)),
        scratch_shapes=[pltpu.VMEM((2, d, two_ff), F32), pltpu.VMEM((2, d_ff, d), F32),
                        pltpu.VMEM((d, two_ff), BF16), pltpu.VMEM((d_ff, d), BF16),
                        pltpu.SemaphoreType.DMA((2, 2))],
    )
    return pl.pallas_call(
        _gmm_body,
        grid_spec=grid_spec,
        out_shape=jax.ShapeDtypeStruct((n_rows, d), F32),
        compiler_params=_params(1),
        name="moe_gmm",
    )(*items, xs, w1, b1.reshape(n_exp, 1, two_ff), w2, b2.reshape(n_exp, 1, d))


def _combine_body(pos_ref, x2_ref, gate_ref, gain_ref, ys_ref, yp_ref, ysm_ref, buf_ref, sem, *, n_prompt_tiles):
    tm = x2_ref.shape[0]

    def issue(t, carry):
        for k in range(TOP_K):
            _row_copy(ys_ref, pos_ref[k, t], buf_ref.at[k], t, sem).start()
        return carry

    lax.fori_loop(0, tm, issue, 0)

    def drain(t, carry):
        for k in range(TOP_K):
            _row_copy(ys_ref, pos_ref[k, t], buf_ref.at[k], t, sem).wait()
        return carry

    lax.fori_loop(0, tm, drain, 0)
    gate = gate_ref[...]
    x3 = x2_ref[...]
    for k in range(TOP_K):
        x3 = x3 + gate[:, k:k + 1] * buf_ref[k]
    y = _rms(x3, gain_ref[...])
    is_prompt = pl.program_id(0) < n_prompt_tiles

    @pl.when(is_prompt)
    def _():
        yp_ref[...] = y

    @pl.when(jnp.logical_not(is_prompt))
    def _():
        ysm_ref[...] = y


def _combine(pos, x2, gate_t, gain, ys, n_prompt_rows):
    n, d = x2.shape
    tm = TOKEN_TILE
    npt = n_prompt_rows // tm
    row = lambda i: (i, 0)
    return pl.pallas_call(
        functools.partial(_combine_body, n_prompt_tiles=npt),
        grid=(n // tm,),
        in_specs=[pl.BlockSpec((TOP_K, tm), lambda i: (0, i), memory_space=pltpu.SMEM),
                  pl.BlockSpec((tm, d), row), pl.BlockSpec((tm, LANES), row), _const_spec(gain.shape),
                  pl.BlockSpec(memory_space=pl.ANY)],
        out_specs=[pl.BlockSpec((tm, d), lambda i: (jnp.minimum(i, npt - 1), 0)),
                   pl.BlockSpec((tm, d), lambda i: (jnp.maximum(i - npt, 0), 0))],
        out_shape=[jax.ShapeDtypeStruct((n_prompt_rows, d), F32),
                   jax.ShapeDtypeStruct((n - n_prompt_rows, d), F32)],
        scratch_shapes=[pltpu.VMEM((TOP_K, tm, d), F32), pltpu.SemaphoreType.DMA(())],
        compiler_params=_params(1),
        name="moe_combine",
    )(pos, x2, gate_t, gain, ys)


def _head_group_lanes(per_head, hps):
    groups = per_head.reshape(-1, 1, hps)
    padded = jnp.pad(groups, ((0, 0), (0, 0), (0, LANES - hps)))
    return jnp.broadcast_to(padded, (groups.shape[0], SUBLANES, LANES)).reshape(-1, LANES)


def kernel(x_prompt, x_sample, mem_prompt, state_gdn, state_gdn_conv, state_conf_conv, cache_mem_k, cache_mem_v,
           norm_mix, w_in, gdn_conv_w, gdn_a_log, gdn_dt_bias, gdn_norm, gdn_o, conf_conv_w, conf_conv_b,
           conf_ln_g, conf_ln_b, conf_pw2, conf_pw2_b, w_out, norm_xa, norm_mem, xa_q, xa_kv, xa_o,
           norm_ffn, router_w, router_b, moe_w1, moe_b1, moe_w2, moe_b2, norm_final):
    n_batch, seq, d = x_prompt.shape
    n_seq, n_tok, _ = x_sample.shape
    depth, _, n_heads, dk, dv = state_gdn.shape
    assert depth == 1 and dk == LANES and dv == LANES
    mem_len, xa_heads = cache_mem_k.shape[2], cache_mem_k.shape[3]
    n_experts = router_w.shape[2]
    qkv_w = gdn_conv_w.shape[2]
    v_w = n_heads * dv
    hps = GDN_HEADS_PER_STEP
    n_hg = n_heads // hps
    n_gc = gdn_conv_w.shape[1] - 1
    n_cc = conf_conv_w.shape[1] - 1
    n_p, n_s = n_batch * seq, n_seq * n_tok
    assert seq % TOKEN_TILE == 0 and n_s % TOKEN_TILE == 0 and n_seq % SEQ_BLOCK == 0
    assert n_tok * SEQ_BLOCK <= LANES and n_heads % hps == 0 and 2 * hps <= LANES
    assert n_gc <= min(n_tok, SUBLANES) and n_cc <= CONF_HALO
    row = lambda a: a.reshape(1, -1)
    to_token_major = lambda a: a.reshape(n_seq, n_tok, -1).transpose(1, 0, 2).reshape(n_s, -1)
    to_seq_major = lambda a: a.reshape(n_tok, n_seq, -1).transpose(1, 0, 2).reshape(n_s, -1)

    xp = x_prompt.reshape(n_p, d)
    xs = to_token_major(x_sample)

    w = w_in.reshape(d, -1).astype(BF16)
    ab_lo, ab_hi = qkv_w + v_w, qkv_w + v_w + 2 * n_heads
    w_a = w[:, ab_lo:ab_lo + n_heads].reshape(d, n_hg, hps)
    w_b = w[:, ab_lo + n_heads:ab_hi].reshape(d, n_hg, hps)
    wab = jnp.pad(jnp.concatenate([w_a, w_b], axis=2), ((0, 0), (0, 0), (0, LANES - 2 * hps))).reshape(d, -1)
    qkv, z, ab, cv, sg = _inproj(xp, xs, row(norm_mix), w[:, :ab_lo], wab, w[:, ab_hi:ab_hi + 2 * d],
                                 w[:, ab_hi + 2 * d:], qkv_w)

    alog = _head_group_lanes(gdn_a_log.reshape(-1), hps)
    dtb = _head_group_lanes(gdn_dt_bias.reshape(-1), hps)
    gnorm = row(gdn_norm)
    og_p, s_p, tail_q, tail_k, tail_v = _gdn_prompt(qkv, z, ab, gdn_conv_w, alog, dtb, gnorm,
                                                    n_batch, seq, n_heads, dk)
    og_s, s_s = _gdn_sample(qkv, z, ab, state_gdn_conv, state_gdn, gdn_conv_w, alog, dtb, gnorm,
                            n_p, n_seq, n_tok, n_heads, dk)

    conf_args = (conf_conv_w, row(conf_conv_b), row(conf_ln_g), row(conf_ln_b))
    yb_p, cconv_tail = _conf_prompt(cv, *conf_args, n_batch, seq)
    yb_s, cconv_s = _conf_sample(cv, state_conf_conv, *conf_args, n_p, n_seq, n_tok)

    bf = lambda a: a.reshape(a.shape[1:]).astype(BF16)
    x1, q = _merge(og_p, og_s, yb_p, yb_s, sg, xp, xs, bf(gdn_o), bf(conf_pw2), row(conf_pw2_b), bf(w_out),
                   row(norm_xa), bf(xa_q))

    mk_p, mv_p = _memkv(mem_prompt.reshape(n_batch * mem_len, d), row(norm_mem), bf(xa_kv))
    o_p = _attn_prompt(q, mk_p, mv_p, n_batch, seq, mem_len, xa_heads)
    o_s = to_token_major(_attn_sample(to_seq_major(q[n_p:]), cache_mem_k, cache_mem_v, n_tok))

    x2, hf, idx, gate_t = _router(o_p, o_s, x1, bf(xa_o), row(norm_ffn), router_w.reshape(d, n_experts).T,
                                  router_b.reshape(-1, 1))
    pos, counts = _positions(idx, n_experts)
    xsorted = _dispatch(pos, hf)
    ysorted = _gmm(xsorted, counts[:, 0], moe_w1, moe_b1, moe_w2, moe_b2)
    y_p, y_s = _combine(pos, x2, gate_t, row(norm_final), ysorted, n_p)

    y_prompt = y_p.reshape(n_batch, seq, d)
    y_sample = to_seq_major(y_s).reshape(n_seq, n_tok, d)
    gconv_p = jnp.concatenate([tail_q, tail_k, tail_v], axis=2)[:, SUBLANES - n_gc:, :]
    gconv_s = to_seq_major(qkv[n_p:]).reshape(n_seq, n_tok, qkv_w)[:, n_tok - n_gc:, :].astype(F32)
    cconv_p = cconv_tail[:, CONF_HALO - n_cc:, :]
    kv_shape = (1, n_batch, mem_len, xa_heads, d // xa_heads)
    return (y_prompt, y_sample, s_p[None], gconv_p[None], cconv_p[None], mk_p.reshape(kv_shape),
            mv_p.reshape(kv_shape), s_s[None], gconv_s[None], cconv_s[None])
```

```python
import functools

import jax
import jax.numpy as jnp
from jax import lax
from jax.experimental import pallas as pl
from jax.experimental.pallas import tpu as pltpu

F32, BF16, I32 = jnp.float32, jnp.bfloat16, jnp.int32
HIGHEST = lax.Precision.HIGHEST

LANES = 128
SUBLANES = 8
V7X_VMEM_REQUEST = 56 * 1024 * 1024

TOKEN_TILE = 512
GDN_CHUNK = 64
GDN_GROUP = 256
GDN_HEADS_PER_STEP = 8
SEQ_BLOCK = 16
CONF_HALO = 32
CONF_ROWS = 32
CONF_LANE_SPLIT = 8
CONF_NORM_ROWS = 128
MOE_ROW_TILE = 256
CUMSUM_CHUNK = 512

TOP_K = 4
SWIGLU_ALPHA = 1.702
SWIGLU_LIMIT = 7.0
RMS_EPS = 1e-6
LN_EPS = 1e-5
L2_EPS = 1e-6


def _params(n_axes):
    return pltpu.CompilerParams(dimension_semantics=("arbitrary",) * n_axes, vmem_limit_bytes=V7X_VMEM_REQUEST)


def _const_spec(shape):
    zeros = (0,) * len(shape)
    return pl.BlockSpec(shape, lambda *_: zeros, pipeline_mode=pl.Buffered(1))


def _layer_spec(shape):
    zeros = (0,) * len(shape)
    return pl.BlockSpec((None,) + tuple(shape[1:]), lambda *_: zeros, pipeline_mode=pl.Buffered(1))


def _mm(a, b):
    return jnp.dot(a.astype(BF16), b.astype(BF16), preferred_element_type=F32)


def _mm_nt(a, b):
    return lax.dot_general(a.astype(BF16), b.astype(BF16), (((1,), (1,)), ((), ())), preferred_element_type=F32)


def _rms(x, gain):
    return x * lax.rsqrt(jnp.mean(x * x, axis=-1, keepdims=True) + RMS_EPS) * gain


def _silu(x):
    return x * jax.nn.sigmoid(x)


def _softplus(x):
    return jnp.maximum(x, 0.0) + jnp.log1p(jnp.exp(-jnp.abs(x)))


def _split_hi_lo(x):
    hi = x.astype(BF16)
    lo = (x - hi.astype(F32)).astype(BF16)
    return hi, lo


def _inproj_body(xp_ref, xs_ref, gain_ref, wqkvz_ref, wab_ref, wglu_ref, wgate_ref,
                 qkv_ref, z_ref, ab_ref, cv_ref, sg_ref, *, n_prompt_tiles):
    i = pl.program_id(0)
    d = xp_ref.shape[1]
    x = jnp.where(i < n_prompt_tiles, xp_ref[...], xs_ref[...])
    h = _rms(x, gain_ref[...]).astype(BF16)
    qkv_w = qkv_ref.shape[1]
    for c in range(0, qkv_w, d):
        qkv_ref[:, c:c + d] = jnp.dot(h, wqkvz_ref[:, c:c + d], preferred_element_type=F32).astype(BF16)
    z_ref[...] = jnp.dot(h, wqkvz_ref[:, qkv_w:], preferred_element_type=F32).astype(BF16)
    ab_ref[...] = jnp.dot(h, wab_ref[...], preferred_element_type=F32)
    glu_a = jnp.dot(h, wglu_ref[:, :d], preferred_element_type=F32)
    glu_b = jnp.dot(h, wglu_ref[:, d:], preferred_element_type=F32)
    cv_ref[...] = glu_a * jax.nn.sigmoid(glu_b)
    for c in range(0, 2 * d, d):
        sg_ref[:, c:c + d] = jax.nn.sigmoid(
            jnp.dot(h, wgate_ref[:, c:c + d], preferred_element_type=F32)).astype(BF16)


def _inproj(xp, xs, gain, wqkvz, wab, wglu, wgate, qkv_w):
    n_p, d = xp.shape
    n_s = xs.shape[0]
    tm = TOKEN_TILE
    npt, nst = n_p // tm, n_s // tm
    n = n_p + n_s
    ab_w = wab.shape[1]
    row = lambda i: (i, 0)
    return pl.pallas_call(
        functools.partial(_inproj_body, n_prompt_tiles=npt),
        grid=(npt + nst,),
        in_specs=[
            pl.BlockSpec((tm, d), lambda i: (jnp.minimum(i, npt - 1), 0)),
            pl.BlockSpec((tm, d), lambda i: (jnp.maximum(i - npt, 0), 0)),
            _const_spec(gain.shape), _const_spec(wqkvz.shape), _const_spec(wab.shape),
            _const_spec(wglu.shape), _const_spec(wgate.shape),
        ],
        out_specs=[
            pl.BlockSpec((tm, qkv_w), row), pl.BlockSpec((tm, d), row), pl.BlockSpec((tm, ab_w), row),
            pl.BlockSpec((tm, d), row), pl.BlockSpec((tm, 2 * d), row),
        ],
        out_shape=[
            jax.ShapeDtypeStruct((n, qkv_w), BF16), jax.ShapeDtypeStruct((n, d), BF16),
            jax.ShapeDtypeStruct((n, ab_w), F32), jax.ShapeDtypeStruct((n, d), F32),
            jax.ShapeDtypeStruct((n, 2 * d), BF16),
        ],
        compiler_params=_params(1),
        name="inproj",
    )(xp, xs, gain, wqkvz, wab, wglu, wgate)


def _decay_scalars(ab, alog_row, dtb_row):
    return -jnp.exp(alog_row) * _softplus(ab + dtb_row), jax.nn.sigmoid(ab)


def _cumsum_rows(incl_bf16, g):
    hi, lo = _split_hi_lo(g)
    both = jnp.dot(incl_bf16, jnp.concatenate([hi, lo], axis=1), preferred_element_type=F32)
    return both[:, :LANES] + both[:, LANES:]


def _l2norm(x):
    return x * lax.rsqrt(jnp.sum(x * x, axis=-1, keepdims=True) + L2_EPS)


def _pair_masks(r, same):
    ri = lax.broadcasted_iota(I32, (r, r), 0)
    ci = lax.broadcasted_iota(I32, (r, r), 1)
    pair = same(ri, ci)
    as_f = lambda m: jnp.where(m, 1.0, 0.0).astype(F32)
    return as_f(ri == ci), as_f(pair & (ri >= ci)), -as_f(pair & (ri > ci))


def _lane_col(x, lane):
    return jnp.broadcast_to(x[:, lane:lane + 1], (x.shape[0], LANES))


def _wy_heads(q, k, v, g_all, beta_all, hps, eye, incl, neg_strict, n_doublings):
    r = q[0].shape[0]
    heads = range(len(q))
    gc_all = _cumsum_rows(incl.astype(BF16), g_all)
    pad = jnp.zeros((max(LANES - r, 0), LANES), F32)
    gc_t = jnp.transpose(jnp.concatenate([gc_all, pad], axis=0) if r < LANES else gc_all)
    gc = [_lane_col(gc_all, h) for h in heads]
    beta = [_lane_col(beta_all, hps + h) for h in heads]
    pair_decay = []
    for h in heads:
        gc_wide = jnp.concatenate([gc[h]] * (r // LANES), axis=1) if r > LANES else gc[h][:, :r]
        pair_decay.append(jnp.exp(jnp.minimum(gc_wide - gc_t[h:h + 1, :r], 0.0)))
    kb = [k[h] * beta[h] for h in heads]
    gram = [_mm_nt(jnp.concatenate([kb[h], q[h]], axis=0), k[h]) for h in heads]
    neg_m = [gram[h][:r] * (pair_decay[h] * neg_strict) for h in heads]
    p = [gram[h][r:] * (pair_decay[h] * incl) for h in heads]
    t_inv = [eye + neg_m[h] for h in heads]
    power = neg_m
    for _ in range(n_doublings):
        power = [_mm(power[h], power[h]) for h in heads]
        t_inv = [t_inv[h] + _mm(t_inv[h], power[h]) for h in heads]
    sol = [_mm(t_inv[h], jnp.concatenate([v[h] * beta[h], kb[h] * jnp.exp(gc[h])], axis=1)) for h in heads]
    return gc, [s[:, :LANES] for s in sol], [s[:, LANES:] for s in sol], p


def _gated_out(o, gnorm, z):
    on = o * lax.rsqrt(jnp.mean(o * o, axis=-1, keepdims=True) + RMS_EPS) * gnorm
    return (on * _silu(z)).astype(BF16)


def _gdn_prompt_body(q_ref, k_ref, v_ref, z_ref, ab_ref, cwq_ref, cwk_ref, cwv_ref, alog_ref, dtb_ref,
                     gnorm_ref, og_ref, sout_ref, tq_ref, tk_ref, tv_ref, halo_ref, s_ref, mask_ref, *, n_taps):
    hps, dk = s_ref.shape[0], s_ref.shape[1]
    grp, chunk = GDN_GROUP, GDN_CHUNK
    n_groups = q_ref.shape[0] // grp
    halo_rows = halo_ref.shape[1]
    halo_ref[...] = jnp.zeros(halo_ref.shape, F32)
    s_ref[...] = jnp.zeros(s_ref.shape, F32)
    for i, m in enumerate(_pair_masks(grp, lambda ri, ci: (ri // chunk) == (ci // chunk))):
        mask_ref[i] = m
    heads = range(hps)

    def group(gi, carry):
        rows = pl.ds(pl.multiple_of(gi * grp, grp), grp)
        conv = []
        for part, (x_ref, cw_ref) in enumerate(((q_ref, cwq_ref), (k_ref, cwk_ref), (v_ref, cwv_ref))):
            xg = x_ref[rows, :].astype(F32)
            xw = jnp.concatenate([halo_ref[part], xg], axis=0)
            halo_ref[part] = xg[grp - halo_rows:, :]
            y = None
            for j in range(n_taps):
                off = halo_rows - (n_taps - 1) + j
                term = cw_ref[j:j + 1, :] * xw[off:off + grp, :]
                y = term if y is None else y + term
            conv.append(_silu(y))
        sl = [slice(h * dk, (h + 1) * dk) for h in heads]
        q = [_l2norm(conv[0][:, sl[h]]) * (dk ** -0.5) for h in heads]
        k = [_l2norm(conv[1][:, sl[h]]) for h in heads]
        v = [conv[2][:, sl[h]] for h in heads]
        g_all, beta_all = _decay_scalars(ab_ref[rows, :], alog_ref[0:1, :], dtb_ref[0:1, :])
        gc, u, w, p = _wy_heads(q, k, v, g_all, beta_all, hps, mask_ref[0], mask_ref[1], mask_ref[2], 5)
        qe = [q[h] * jnp.exp(gc[h]) for h in heads]
        state = [s_ref[h] for h in heads]
        outs = [[] for _ in heads]
        for j in range(grp // chunk):
            cs = slice(j * chunk, (j + 1) * chunk)
            g_last = [gc[h][(j + 1) * chunk - 1:(j + 1) * chunk, :] for h in heads]
            k_dec = [k[h][cs] * jnp.exp(g_last[h] - gc[h][cs]) for h in heads]
            ws_qs = [_mm(jnp.concatenate([w[h][cs], qe[h][cs]], axis=0), state[h]) for h in heads]
            v_new = [u[h][cs] - ws_qs[h][:chunk] for h in heads]
            for h in heads:
                outs[h].append(ws_qs[h][chunk:] + _mm(p[h][cs, cs], v_new[h]))
            state = [state[h] * jnp.exp(g_last[h]) + lax.dot_general(
                k_dec[h].astype(BF16), v_new[h].astype(BF16), (((0,), (0,)), ((), ())),
                preferred_element_type=F32) for h in heads]
        for h in heads:
            s_ref[h] = state[h]
            o = jnp.concatenate(outs[h], axis=0)
            og_ref[rows, sl[h]] = _gated_out(o, gnorm_ref[...], z_ref[rows, sl[h]].astype(F32))
        return carry

    lax.fori_loop(0, n_groups, group, 0)
    sout_ref[0] = s_ref[...]
    tq_ref[0] = halo_ref[0]
    tk_ref[0] = halo_ref[1]
    tv_ref[0] = halo_ref[2]


def _gdn_prompt(qkv, z, ab, conv_w, alog, dtb, gnorm, n_batch, seq, n_heads, dk):
    hps = GDN_HEADS_PER_STEP
    hgw = hps * dk
    n_hg = n_heads // hps
    n_taps = conv_w.shape[1]
    grp = GDN_GROUP
    col = lambda part: (lambda b, h: (b, part * n_hg + h))
    cwcol = lambda part: (lambda b, h: (0, 0, part * n_hg + h))
    bh = lambda b, h: (b, h)
    tail = jax.ShapeDtypeStruct((n_batch, SUBLANES, n_heads * dk), F32)
    tail_spec = pl.BlockSpec((1, SUBLANES, hgw), lambda b, h: (b, 0, h))
    return pl.pallas_call(
        functools.partial(_gdn_prompt_body, n_taps=n_taps),
        grid=(n_batch, n_hg),
        in_specs=[
            pl.BlockSpec((seq, hgw), col(0)), pl.BlockSpec((seq, hgw), col(1)), pl.BlockSpec((seq, hgw), col(2)),
            pl.BlockSpec((seq, hgw), bh),
            pl.BlockSpec((seq, LANES), bh),
            pl.BlockSpec((None, n_taps, hgw), cwcol(0)), pl.BlockSpec((None, n_taps, hgw), cwcol(1)),
            pl.BlockSpec((None, n_taps, hgw), cwcol(2)),
            pl.BlockSpec((SUBLANES, LANES), lambda b, h: (h, 0)),
            pl.BlockSpec((SUBLANES, LANES), lambda b, h: (h, 0)),
            _const_spec(gnorm.shape),
        ],
        out_specs=[
            pl.BlockSpec((seq, hgw), bh),
            pl.BlockSpec((1, hps, dk, dk), lambda b, h: (b, h, 0, 0)),
            tail_spec, tail_spec, tail_spec,
        ],
        out_shape=[
            jax.ShapeDtypeStruct((n_batch * seq, n_heads * dk), BF16),
            jax.ShapeDtypeStruct((n_batch, n_heads, dk, dk), F32),
            tail, tail, tail,
        ],
        scratch_shapes=[pltpu.VMEM((3, SUBLANES, hgw), F32), pltpu.VMEM((hps, dk, dk), F32),
                        pltpu.VMEM((3, grp, grp), F32)],
        compiler_params=_params(2),
        name="gdn_prompt",
    )(qkv, qkv, qkv, z, ab, conv_w, conv_w, conv_w, alog, dtb, gnorm)


def _gdn_sample_body(*refs, n_tok, n_taps):
    sb = SEQ_BLOCK
    it = iter(refs)
    take = lambda n: [next(it) for _ in range(n)]
    x_refs = [take(n_tok) for _ in range(3)]
    cb_refs = take(3)
    z_refs = take(n_tok)
    ab_refs = take(n_tok)
    s0_ref = next(it)
    cw_refs = take(3)
    alog_ref, dtb_ref, gnorm_ref = take(3)
    og_refs = take(n_tok)
    sout_ref = next(it)

    hps, dk = s0_ref.shape[1], s0_ref.shape[2]
    heads = range(hps)
    r = n_tok * sb

    conv = []
    for part in range(3):
        ext = [cb_refs[part][:, j, :] for j in range(n_taps - 1)]
        ext += [x_refs[part][t][...].astype(F32) for t in range(n_tok)]
        ys = []
        for t in range(n_tok):
            y = None
            for j in range(n_taps):
                term = cw_refs[part][j:j + 1, :] * ext[t + j]
                y = term if y is None else y + term
            ys.append(_silu(y))
        conv.append(jnp.concatenate(ys, axis=0))
    ab = jnp.concatenate([a[...] for a in ab_refs], axis=0)

    eye, incl, neg_strict = _pair_masks(r, lambda ri, ci: (ri % sb) == (ci % sb))
    row_seq = lax.broadcasted_iota(I32, (2 * r, dk), 0) % sb
    col_seq = lax.broadcasted_iota(I32, (dk, LANES), 1) % sb
    n_doublings = max((n_tok - 1).bit_length() - 1, 0)

    sl = [slice(h * dk, (h + 1) * dk) for h in heads]
    q = [_l2norm(conv[0][:, sl[h]]) * (dk ** -0.5) for h in heads]
    k = [_l2norm(conv[1][:, sl[h]]) for h in heads]
    v = [conv[2][:, sl[h]] for h in heads]
    g_all, beta_all = _decay_scalars(ab, alog_ref[0:1, :], dtb_ref[0:1, :])
    gc, u, w, p = _wy_heads(q, k, v, g_all, beta_all, hps, eye, incl, neg_strict, n_doublings)
    for h in heads:
        wq = jnp.concatenate([w[h], q[h] * jnp.exp(gc[h])], axis=0).astype(BF16)
        ws_qs = jnp.zeros((2 * r, dk), F32)
        for s in range(sb):
            ws_qs = jnp.where(row_seq == s, _mm(wq, s0_ref[s, h]), ws_qs)
        v_new = u[h] - ws_qs[:r]
        o = ws_qs[r:] + _mm(p[h], v_new)
        z = jnp.concatenate([zr[:, sl[h]] for zr in z_refs], axis=0).astype(F32)
        og = _gated_out(o, gnorm_ref[...], z)
        for t in range(n_tok):
            og_refs[t][:, sl[h]] = og[t * sb:(t + 1) * sb]
        g_end = gc[h][r - sb:, :]
        k_dec = k[h] * jnp.exp(jnp.concatenate([g_end] * n_tok, axis=0) - gc[h])
        k_dec_t = jnp.transpose(jnp.concatenate([k_dec, jnp.zeros((LANES - r, dk), F32)], axis=0))
        v_new_pad = jnp.concatenate([v_new, jnp.zeros((LANES - r, dk), F32)], axis=0).astype(BF16)
        for s in range(sb):
            upd = _mm(jnp.where(col_seq == s, k_dec_t, 0.0), v_new_pad)
            sout_ref[s, h] = s0_ref[s, h] * jnp.exp(g_end[s:s + 1, :]) + upd


def _gdn_sample(qkv, z, ab, conv_buf, s0, conv_w, alog, dtb, gnorm, n_prompt_rows, n_seq, n_tok, n_heads, dk):
    sb, hps = SEQ_BLOCK, GDN_HEADS_PER_STEP
    hgw = hps * dk
    n_hg = n_heads // hps
    n_taps = conv_w.shape[1]
    base = n_prompt_rows // sb
    per_t = n_seq // sb

    def tok_rows(t, colfn):
        return lambda i, h: (base + t * per_t + i, colfn(h))

    in_specs, args = [], []
    for part in range(3):
        for t in range(n_tok):
            in_specs.append(pl.BlockSpec((sb, hgw), tok_rows(t, lambda h, part=part: part * n_hg + h)))
            args.append(qkv)
    for part in range(3):
        in_specs.append(pl.BlockSpec((None, sb, n_taps - 1, hgw),
                                     lambda i, h, part=part: (0, i, 0, part * n_hg + h)))
        args.append(conv_buf)
    for t in range(n_tok):
        in_specs.append(pl.BlockSpec((sb, hgw), tok_rows(t, lambda h: h)))
        args.append(z)
    for t in range(n_tok):
        in_specs.append(pl.BlockSpec((sb, LANES), tok_rows(t, lambda h: h)))
        args.append(ab)
    in_specs.append(pl.BlockSpec((None, sb, hps, dk, dk), lambda i, h: (0, i, h, 0, 0)))
    args.append(s0)
    for part in range(3):
        in_specs.append(pl.BlockSpec((None, n_taps, hgw), lambda i, h, part=part: (0, 0, part * n_hg + h)))
        args.append(conv_w)
    in_specs += [pl.BlockSpec((SUBLANES, LANES), lambda i, h: (h, 0)),
                 pl.BlockSpec((SUBLANES, LANES), lambda i, h: (h, 0)), _const_spec(gnorm.shape)]
    args += [alog, dtb, gnorm]
    outs = pl.pallas_call(
        functools.partial(_gdn_sample_body, n_tok=n_tok, n_taps=n_taps),
        grid=(per_t, n_hg),
        in_specs=in_specs,
        out_specs=[pl.BlockSpec((sb, hgw), lambda i, h: (i, h))] * n_tok
        + [pl.BlockSpec((sb, hps, dk, dk), lambda i, h: (i, h, 0, 0))],
        out_shape=[jax.ShapeDtypeStruct((n_seq, n_heads * dk), BF16)] * n_tok
        + [jax.ShapeDtypeStruct(s0.shape[1:], F32)],
        compiler_params=_params(2),
        name="gdn_sample",
    )(*args)
    return jnp.concatenate(outs[:n_tok], axis=0), outs[n_tok]


def _ln_silu(y, lg, lb):
    mu = jnp.mean(y, axis=-1, keepdims=True)
    yc = y - mu
    var = jnp.mean(yc * yc, axis=-1, keepdims=True)
    return _silu(yc * lax.rsqrt(var + LN_EPS) * lg + lb).astype(BF16)


def _conf_prompt_body(cv_ref, halo_ref, cw_ref, cb_ref, lg_ref, lb_ref, out_ref, tail_ref, rot_ref, conv_ref,
                      *, n_taps):
    tile_in_seq = pl.program_id(1)
    tt, c = cv_ref.shape
    halo = CONF_HALO
    rot_ref[0, 0:halo, :] = jnp.where(tile_in_seq == 0, 0.0, halo_ref[...])
    rot_ref[0, halo:halo + tt, :] = cv_ref[...]
    rot_ref[0, halo + tt:halo + tt + SUBLANES, :] = jnp.zeros((SUBLANES, c), F32)
    step = 32

    def rotate(i, carry):
        base = pl.multiple_of(i * step, step)
        win = rot_ref[0, pl.ds(base, step + SUBLANES), :]
        for r in range(1, SUBLANES):
            rot_ref[r, pl.ds(base, step), :] = win[r:r + step, :]
        return carry

    lax.fori_loop(0, (halo + tt) // step, rotate, 0)
    lead = halo - (n_taps - 1)
    rows = CONF_ROWS
    groups = rows // SUBLANES
    lane_w = c // CONF_LANE_SPLIT
    by_rot = [[(j, (lead + j) // SUBLANES) for j in range(n_taps) if (lead + j) % SUBLANES == r]
              for r in range(SUBLANES)]

    for lh in range(CONF_LANE_SPLIT):
        ls = slice(lh * lane_w, (lh + 1) * lane_w)

        def conv(i, carry, ls=ls):
            base = pl.multiple_of(i * rows, rows)
            accs = [jnp.broadcast_to(cb_ref[:, ls], (SUBLANES, lane_w))] * groups
            for r, taps in enumerate(by_rot):
                if not taps:
                    continue
                lo = min(a for _, a in taps)
                hi = max(a for _, a in taps) + groups
                slabs = {m: rot_ref[r, pl.ds(pl.multiple_of(base + m * SUBLANES, SUBLANES), SUBLANES), ls]
                         for m in range(lo, hi)}
                for j, a in taps:
                    w8 = cw_ref[j, :, ls]
                    accs = [acc + w8 * slabs[a + g] for g, acc in enumerate(accs)]
            conv_ref[pl.ds(base, rows), ls] = jnp.concatenate(accs, axis=0)
            return carry

        lax.fori_loop(0, tt // rows, conv, 0)

    nrows = CONF_NORM_ROWS

    def norm(i, carry):
        base = pl.multiple_of(i * nrows, nrows)
        out_ref[pl.ds(base, nrows), :] = _ln_silu(conv_ref[pl.ds(base, nrows), :], lg_ref[...], lb_ref[...])
        return carry

    lax.fori_loop(0, tt // nrows, norm, 0)

    @pl.when(tile_in_seq == pl.num_programs(1) - 1)
    def _():
        tail_ref[0] = cv_ref[tt - halo:, :]


def _conf_prompt(cv, cw, cb, lg, lb, n_batch, seq):
    c = cv.shape[1]
    tt = TOKEN_TILE
    tiles = seq // tt
    halo_per_tile = tt // CONF_HALO
    n_taps = cw.shape[1]
    cw8 = jnp.broadcast_to(cw.reshape(n_taps, 1, c), (n_taps, SUBLANES, c))
    return pl.pallas_call(
        functools.partial(_conf_prompt_body, n_taps=n_taps),
        grid=(n_batch, tiles),
        in_specs=[
            pl.BlockSpec((tt, c), lambda b, j: (b * tiles + j, 0)),
            pl.BlockSpec((CONF_HALO, c), lambda b, j: (jnp.maximum((b * tiles + j) * halo_per_tile - 1, 0), 0)),
            _const_spec(cw8.shape), _const_spec(cb.shape), _const_spec(lg.shape), _const_spec(lb.shape),
        ],
        out_specs=[pl.BlockSpec((tt, c), lambda b, j: (b * tiles + j, 0)),
                   pl.BlockSpec((1, CONF_HALO, c), lambda b, j: (b, 0, 0))],
        out_shape=[jax.ShapeDtypeStruct((n_batch * seq, c), BF16),
                   jax.ShapeDtypeStruct((n_batch, CONF_HALO, c), F32)],
        scratch_shapes=[pltpu.VMEM((SUBLANES, CONF_HALO + tt + SUBLANES, c), F32), pltpu.VMEM((tt, c), F32)],
        compiler_params=_params(2),
        name="conf_prompt",
    )(cv, cv, cw8, cb, lg, lb)


def _conf_sample_body(*refs, n_tok, n_taps):
    buf_ref = refs[0]
    cv_refs = refs[1:1 + n_tok]
    cw_ref, cb_ref, lg_ref, lb_ref = refs[1 + n_tok:5 + n_tok]
    out_refs = refs[5 + n_tok:5 + 2 * n_tok]
    newbuf_ref = refs[5 + 2 * n_tok]
    n_hist = n_taps - 1

    def ext(i):
        return buf_ref[:, i, :] if i < n_hist else cv_refs[i - n_hist][...]

    for t in range(n_tok):
        acc = jnp.broadcast_to(cb_ref[...], cv_refs[0].shape)
        for j in range(n_taps):
            acc = acc + cw_ref[j:j + 1, :] * ext(t + j)
        out_refs[t][...] = _ln_silu(acc, lg_ref[...], lb_ref[...])
    for i in range(n_hist):
        newbuf_ref[:, i, :] = ext(i + n_tok)


def _conf_sample(cv, buf, cw, cb, lg, lb, n_prompt_rows, n_seq, n_tok):
    sb = SEQ_BLOCK
    c = cv.shape[1]
    n_taps = cw.shape[1]
    base, per_t = n_prompt_rows // sb, n_seq // sb
    in_specs = [pl.BlockSpec((None, sb, n_taps - 1, c), lambda i: (0, i, 0, 0))]
    in_specs += [pl.BlockSpec((sb, c), lambda i, t=t: (base + t * per_t + i, 0)) for t in range(n_tok)]
    in_specs += [_layer_spec(cw.shape), _const_spec(cb.shape), _const_spec(lg.shape), _const_spec(lb.shape)]
    outs = pl.pallas_call(
        functools.partial(_conf_sample_body, n_tok=n_tok, n_taps=n_taps),
        grid=(per_t,),
        in_specs=in_specs,
        out_specs=[pl.BlockSpec((sb, c), lambda i: (i, 0))] * n_tok
        + [pl.BlockSpec((sb, n_taps - 1, c), lambda i: (i, 0, 0))],
        out_shape=[jax.ShapeDtypeStruct((n_seq, c), BF16)] * n_tok
        + [jax.ShapeDtypeStruct((n_seq, n_taps - 1, c), F32)],
        compiler_params=_params(1),
        name="conf_sample",
    )(buf, *([cv] * n_tok), cw, cb, lg, lb)
    return jnp.concatenate(outs[:n_tok], axis=0), outs[n_tok]


def _merge_body(ogp_ref, ogs_ref, ybp_ref, ybs_ref, sg_ref, xp_ref, xs_ref, wgo_ref, wpw_ref, bpw_ref,
                wout_ref, gxa_ref, wq_ref, x1_ref, q_ref, *, n_prompt_tiles):
    is_prompt = pl.program_id(0) < n_prompt_tiles
    d = xp_ref.shape[1]
    og = jnp.where(is_prompt, ogp_ref[...], ogs_ref[...])
    yb_in = jnp.where(is_prompt, ybp_ref[...], ybs_ref[...])
    x = jnp.where(is_prompt, xp_ref[...], xs_ref[...])
    y_a = jnp.dot(og, wgo_ref[...], preferred_element_type=F32)
    y_b = jnp.dot(yb_in, wpw_ref[...], preferred_element_type=F32) + bpw_ref[...]
    mixed = sg_ref[:, :d].astype(F32) * y_a + sg_ref[:, d:].astype(F32) * y_b
    x1 = x + jnp.dot(mixed.astype(BF16), wout_ref[...], preferred_element_type=F32)
    x1_ref[...] = x1
    q_ref[...] = jnp.dot(_rms(x1, gxa_ref[...]).astype(BF16), wq_ref[...],
                         preferred_element_type=F32).astype(BF16)


def _dual(tm, cols, npt):
    return [pl.BlockSpec((tm, cols), lambda i: (jnp.minimum(i, npt - 1), 0)),
            pl.BlockSpec((tm, cols), lambda i: (jnp.maximum(i - npt, 0), 0))]


def _merge(og_p, og_s, yb_p, yb_s, sg, xp, xs, wgo, wpw, bpw, wout, gxa, wq):
    n_p, d = xp.shape
    tm = TOKEN_TILE
    npt, nst = n_p // tm, xs.shape[0] // tm
    n = n_p + xs.shape[0]
    row = lambda i: (i, 0)
    return pl.pallas_call(
        functools.partial(_merge_body, n_prompt_tiles=npt),
        grid=(npt + nst,),
        in_specs=_dual(tm, d, npt) + _dual(tm, d, npt) + [pl.BlockSpec((tm, 2 * d), row)] + _dual(tm, d, npt)
        + [_const_spec(a.shape) for a in (wgo, wpw, bpw, wout, gxa, wq)],
        out_specs=[pl.BlockSpec((tm, d), row), pl.BlockSpec((tm, d), row)],
        out_shape=[jax.ShapeDtypeStruct((n, d), F32), jax.ShapeDtypeStruct((n, d), BF16)],
        compiler_params=_params(1),
        name="merge",
    )(og_p, og_s, yb_p, yb_s, sg, xp, xs, wgo, wpw, bpw, wout, gxa, wq)


def _memkv_body(m_ref, gain_ref, w_ref, k_ref, v_ref):
    d = k_ref.shape[1]
    h = _rms(m_ref[...], gain_ref[...]).astype(BF16)
    k_ref[...] = jnp.dot(h, w_ref[:, :d], preferred_element_type=F32)
    v_ref[...] = jnp.dot(h, w_ref[:, d:], preferred_element_type=F32)


def _memkv(mem, gain, w):
    n, d = mem.shape
    tm = min(TOKEN_TILE, n)
    row = lambda i: (i, 0)
    return pl.pallas_call(
        _memkv_body,
        grid=(n // tm,),
        in_specs=[pl.BlockSpec((tm, d), row), _const_spec(gain.shape), _const_spec(w.shape)],
        out_specs=[pl.BlockSpec((tm, d), row)] * 2,
        out_shape=[jax.ShapeDtypeStruct((n, d), F32)] * 2,
        compiler_params=_params(1),
        name="memkv",
    )(mem, gain, w)


def _softmax_rows(s):
    e = jnp.exp(s - jnp.max(s, axis=-1, keepdims=True))
    return e / jnp.sum(e, axis=-1, keepdims=True)


def _attn_prompt_body(q_ref, k_ref, v_ref, o_ref, *, n_heads):
    dh = q_ref.shape[1] // n_heads
    for h in range(n_heads):
        sl = slice(h * dh, (h + 1) * dh)
        s = _mm_nt(q_ref[:, sl], k_ref[:, sl]) * (dh ** -0.5)
        o_ref[:, sl] = _mm(_softmax_rows(s), v_ref[:, sl]).astype(BF16)


def _attn_prompt(q, k, v, n_batch, seq, mem_len, n_heads):
    d = q.shape[1]
    tq = TOKEN_TILE
    tiles = seq // tq
    return pl.pallas_call(
        functools.partial(_attn_prompt_body, n_heads=n_heads),
        grid=(n_batch, tiles),
        in_specs=[
            pl.BlockSpec((tq, d), lambda b, j: (b * tiles + j, 0)),
            pl.BlockSpec((mem_len, d), lambda b, j: (b, 0)),
            pl.BlockSpec((mem_len, d), lambda b, j: (b, 0)),
        ],
        out_specs=pl.BlockSpec((tq, d), lambda b, j: (b * tiles + j, 0)),
        out_shape=jax.ShapeDtypeStruct((n_batch * seq, d), BF16),
        compiler_params=_params(2),
        name="attn_prompt",
    )(q, k, v)


def _attn_sample_body(q_ref, k_hbm, v_hbm, o_ref, kbuf, vbuf, sem):
    _, sb, mem_len, dh = kbuf.shape
    n_heads = k_hbm.shape[3]
    rows_per_seq = q_ref.shape[0] // sb
    blk = pl.program_id(0)

    def head_copies(block, h):
        seqs = pl.ds(block * sb, sb)
        return (pltpu.make_async_copy(k_hbm.at[0, seqs, :, h, :], kbuf.at[h % 2], sem.at[0, h % 2]),
                pltpu.make_async_copy(v_hbm.at[0, seqs, :, h, :], vbuf.at[h % 2], sem.at[1, h % 2]))

    def start(block, h):
        for cp in head_copies(block, h):
            cp.start()

    @pl.when(blk == 0)
    def _():
        start(blk, 0)

    for h in range(n_heads):
        if h + 1 < n_heads:
            start(blk, h + 1)
        else:
            @pl.when(blk + 1 < pl.num_programs(0))
            def _():
                start(blk + 1, 0)
        for cp in head_copies(blk, h):
            cp.wait()
        sl = slice(h * dh, (h + 1) * dh)
        keys = kbuf[h % 2].reshape(sb * mem_len, dh)
        vals = vbuf[h % 2].reshape(sb * mem_len, dh)
        s = _mm_nt(q_ref[:, sl], keys) * (dh ** -0.5)
        row_seq = lax.broadcasted_iota(I32, s.shape, 0) // rows_per_seq
        col_seq = lax.broadcasted_iota(I32, s.shape, 1) // mem_len
        p = _softmax_rows(jnp.where(row_seq == col_seq, s, -jnp.inf))
        o_ref[:, sl] = _mm(p, vals).astype(BF16)


def _attn_sample(q_seq_major, kc, vc, n_tok):
    n_s, d = q_seq_major.shape
    _, _, mem_len, n_heads, dh = kc.shape
    assert n_heads % 2 == 0
    sb = SEQ_BLOCK
    rows = sb * n_tok
    return pl.pallas_call(
        _attn_sample_body,
        grid=(n_s // rows,),
        in_specs=[pl.BlockSpec((rows, d), lambda i: (i, 0)), pl.BlockSpec(memory_space=pl.ANY),
                  pl.BlockSpec(memory_space=pl.ANY)],
        out_specs=pl.BlockSpec((rows, d), lambda i: (i, 0)),
        out_shape=jax.ShapeDtypeStruct((n_s, d), BF16),
        scratch_shapes=[pltpu.VMEM((2, sb, mem_len, dh), F32), pltpu.VMEM((2, sb, mem_len, dh), F32),
                        pltpu.SemaphoreType.DMA((2, 2))],
        compiler_params=_params(1),
        name="attn_sample",
    )(q_seq_major, kc, vc)


def _router_body(op_ref, os_ref, x1_ref, wo_ref, gain_ref, rwt_ref, rb_ref,
                 x2_ref, hf_ref, idx_ref, gate_ref, *, n_prompt_tiles):
    o = jnp.where(pl.program_id(0) < n_prompt_tiles, op_ref[...], os_ref[...])
    x2 = x1_ref[...] + jnp.dot(o, wo_ref[...], preferred_element_type=F32)
    x2_ref[...] = x2
    hf = _rms(x2, gain_ref[...])
    hf_ref[...] = hf
    logits = lax.dot_general(rwt_ref[...], hf, (((1,), (1,)), ((), ())), precision=HIGHEST,
                             preferred_element_type=F32) + rb_ref[...]
    n_exp, tm = logits.shape
    expert = lax.broadcasted_iota(I32, logits.shape, 0)
    vals, picks = [], []
    for _ in range(TOP_K):
        best = jnp.max(logits, axis=0, keepdims=True)
        pick = jnp.min(jnp.where(logits == best, expert, n_exp), axis=0, keepdims=True)
        vals.append(best)
        picks.append(pick)
        logits = jnp.where(expert == pick, -jnp.inf, logits)
    idx_ref[...] = jnp.concatenate(picks, axis=0)
    es = [jnp.exp(v - vals[0]) for v in vals]
    total = functools.reduce(lambda a, b: a + b, es)
    gates = jnp.concatenate([e / total for e in es] + [jnp.zeros((LANES - TOP_K, tm), F32)], axis=0)
    gate_ref[...] = jnp.transpose(gates)


def _router(o_p, o_s, x1, wo, gain, rwt, rb):
    n, d = x1.shape
    tm = TOKEN_TILE
    npt = o_p.shape[0] // tm
    row = lambda i: (i, 0)
    return pl.pallas_call(
        functools.partial(_router_body, n_prompt_tiles=npt),
        grid=(n // tm,),
        in_specs=_dual(tm, d, npt) + [pl.BlockSpec((tm, d), row)]
        + [_const_spec(a.shape) for a in (wo, gain, rwt, rb)],
        out_specs=[pl.BlockSpec((tm, d), row), pl.BlockSpec((tm, d), row),
                   pl.BlockSpec((TOP_K, tm), lambda i: (0, i)), pl.BlockSpec((tm, LANES), row)],
        out_shape=[jax.ShapeDtypeStruct((n, d), F32), jax.ShapeDtypeStruct((n, d), F32),
                   jax.ShapeDtypeStruct((TOP_K, n), I32), jax.ShapeDtypeStruct((n, LANES), F32)],
        compiler_params=_params(1),
        name="router",
    )(o_p, o_s, x1, wo, gain, rwt, rb)


def _positions_body(idx_ref, pos_ref, counts_ref, rank_ref, *, n_experts):
    ch = CUMSUM_CHUNK
    n = idx_ref.shape[1]
    n_chunks = n // ch
    upper = (lax.broadcasted_iota(I32, (ch, ch), 0) <= lax.broadcasted_iota(I32, (ch, ch), 1)).astype(BF16)
    expert = lax.broadcasted_iota(I32, (n_experts, ch), 0)

    def onehot(k, c):
        cols = pl.ds(pl.multiple_of(c * ch, ch), ch)
        return cols, expert == idx_ref[k:k + 1, cols]

    running = jnp.zeros((n_experts, 1), F32)
    for k in range(TOP_K):
        def count(c, run, k=k):
            cols, oh = onehot(k, c)
            cum = jnp.dot(oh.astype(BF16), upper, preferred_element_type=F32)
            rank_ref[k:k + 1, cols] = jnp.sum(jnp.where(oh, cum - 1.0 + run, 0.0), axis=0, keepdims=True)
            return run + cum[:, ch - 1:ch]
        running = lax.fori_loop(0, n_chunks, count, running)

    counts = jnp.broadcast_to(running, (n_experts, LANES))
    counts_ref[...] = counts.astype(I32)
    lower = (lax.broadcasted_iota(I32, (n_experts, n_experts), 0)
             > lax.broadcasted_iota(I32, (n_experts, n_experts), 1)).astype(F32)
    start = jnp.dot(lower, counts, precision=HIGHEST, preferred_element_type=F32)[:, 0:1]

    for k in range(TOP_K):
        def place(c, carry, k=k):
            cols, oh = onehot(k, c)
            first = jnp.sum(jnp.where(oh, start, 0.0), axis=0, keepdims=True)
            pos_ref[k:k + 1, cols] = (rank_ref[k:k + 1, cols] + first).astype(I32)
            return carry
        lax.fori_loop(0, n_chunks, place, 0)


def _positions(idx, n_experts):
    k, n = idx.shape
    return pl.pallas_call(
        functools.partial(_positions_body, n_experts=n_experts),
        out_shape=[jax.ShapeDtypeStruct((k, n), I32), jax.ShapeDtypeStruct((n_experts, LANES), I32)],
        scratch_shapes=[pltpu.VMEM((k, n), F32)],
        compiler_params=pltpu.CompilerParams(vmem_limit_bytes=V7X_VMEM_REQUEST),
        name="positions",
    )(idx)


def _row_copy(src_ref, src_row, dst_ref, dst_row, sem):
    return pltpu.make_async_copy(src_ref.at[pl.ds(src_row, 1), :], dst_ref.at[pl.ds(dst_row, 1), :], sem)


def _dispatch_body(pos_ref, hf_ref, xs_ref, sem):
    tm = hf_ref.shape[0]

    def issue(t, carry):
        for k in range(TOP_K):
            _row_copy(hf_ref, t, xs_ref, pos_ref[k, t], sem).start()
        return carry

    lax.fori_loop(0, tm, issue, 0)

    def drain(t, carry):
        for k in range(TOP_K):
            _row_copy(hf_ref, t, xs_ref, pos_ref[k, t], sem).wait()
        return carry

    lax.fori_loop(0, tm, drain, 0)


def _dispatch(pos, hf):
    n, d = hf.shape
    tm = TOKEN_TILE
    return pl.pallas_call(
        _dispatch_body,
        grid=(n // tm,),
        in_specs=[pl.BlockSpec((TOP_K, tm), lambda i: (0, i), memory_space=pltpu.SMEM),
                  pl.BlockSpec((tm, d), lambda i: (i, 0))],
        out_specs=pl.BlockSpec(memory_space=pl.ANY),
        out_shape=jax.ShapeDtypeStruct((TOP_K * n, d), F32),
        scratch_shapes=[pltpu.SemaphoreType.DMA(())],
        compiler_params=_params(1),
        name="dispatch",
    )(pos, hf)


def _gmm_body(tile_ref, exp_ref, lo_ref, hi_ref, first_ref, slot_ref, next_ref, nitems_ref,
              xs_ref, w1_hbm, b1_ref, w2_hbm, b2_ref, ys_ref, w1f_ref, w2f_ref, w1b_ref, w2b_ref, sem):
    w = pl.program_id(0)

    def weight_copies(expert, slot):
        return (pltpu.make_async_copy(w1_hbm.at[0, expert], w1f_ref.at[slot], sem.at[0, slot]),
                pltpu.make_async_copy(w2_hbm.at[0, expert], w2f_ref.at[slot], sem.at[1, slot]))

    @pl.when(w < nitems_ref[0])
    def _():
        slot = slot_ref[w]
        new_expert = jnp.logical_or(w == 0, exp_ref[w] != exp_ref[jnp.maximum(w - 1, 0)])

        @pl.when(w == 0)
        def _():
            for cp in weight_copies(exp_ref[0], slot):
                cp.start()

        @pl.when(new_expert)
        def _():
            @pl.when(next_ref[w] >= 0)
            def _():
                for cp in weight_copies(next_ref[w], 1 - slot):
                    cp.start()

            for cp in weight_copies(exp_ref[w], slot):
                cp.wait()
            w1b_ref[...] = w1f_ref[slot].astype(BF16)
            w2b_ref[...] = w2f_ref[slot].astype(BF16)

        d_ff = w2b_ref.shape[0]
        hid = jnp.dot(xs_ref[...].astype(BF16), w1b_ref[...], preferred_element_type=F32) + b1_ref[0]
        glu = jnp.minimum(hid[:, :d_ff], SWIGLU_LIMIT)
        lin = jnp.clip(hid[:, d_ff:], -SWIGLU_LIMIT, SWIGLU_LIMIT)
        act = glu * jax.nn.sigmoid(SWIGLU_ALPHA * glu) * (lin + 1.0)
        y = jnp.dot(act.astype(BF16), w2b_ref[...], preferred_element_type=F32) + b2_ref[0]
        rows = lax.broadcasted_iota(I32, (ys_ref.shape[0], 1), 0)
        y = jnp.where((rows >= lo_ref[w]) & (rows < hi_ref[w]), y, 0.0)

        @pl.when(first_ref[w] == 1)
        def _():
            ys_ref[...] = y

        @pl.when(first_ref[w] == 0)
        def _():
            ys_ref[...] += y


def _group_work_items(counts, n_items):
    tr = MOE_ROW_TILE
    n_exp = counts.shape[0]
    experts = jnp.arange(n_exp, dtype=I32)
    ends = jnp.cumsum(counts)
    starts = ends - counts
    first_tile = starts // tr
    last_tile = jnp.maximum(ends - 1, 0) // tr
    per_expert = jnp.where(counts > 0, last_tile - first_tile + 1, 0)
    item_end = jnp.cumsum(per_expert)
    total = item_end[-1]
    w = jnp.minimum(jnp.arange(n_items, dtype=I32), total - 1)
    e = jnp.sum((item_end[None, :] <= w[:, None]).astype(I32), axis=1)
    onehot = (e[:, None] == experts[None, :]).astype(I32)
    pick = lambda a: jnp.sum(onehot * a[None, :], axis=1)
    tile = pick(first_tile) + (w - pick(item_end - per_expert))
    valid = jnp.arange(n_items, dtype=I32) < total
    lo = jnp.where(valid, jnp.maximum(pick(starts) - tile * tr, 0), 0)
    hi = jnp.where(valid, jnp.minimum(pick(ends) - tile * tr, tr), 0)
    first = jnp.concatenate([jnp.ones((1,), I32), (tile[1:] != tile[:-1]).astype(I32)])
    later = (experts[None, :] > experts[:, None]) & (per_expert[None, :] > 0)
    next_expert = jnp.min(jnp.where(later, experts[None, :], n_exp), axis=1)
    next_expert = jnp.where(next_expert < n_exp, next_expert, -1)
    rank = jnp.cumsum((per_expert > 0).astype(I32)) - 1
    as_i32 = lambda a: a.astype(I32)
    return (as_i32(tile), as_i32(e), as_i32(lo), as_i32(hi), as_i32(first), as_i32(pick(rank) % 2),
            as_i32(pick(next_expert)), as_i32(total).reshape(1))


def _gmm(xs, counts, w1, b1, w2, b2):
    n_rows, d = xs.shape
    _, n_exp, _, two_ff = w1.shape
    d_ff = two_ff // 2
    tr = MOE_ROW_TILE
    n_items = n_rows // tr + n_exp - 1
    items = _group_work_items(counts, n_items)
    grid_spec = pltpu.PrefetchScalarGridSpec(
        num_scalar_prefetch=len(items),
        grid=(n_items,),
        in_specs=[
            pl.BlockSpec((tr, d), lambda w, tile, *_: (tile[w], 0)),
            pl.BlockSpec(memory_space=pl.ANY),
            pl.BlockSpec((1, 1, two_ff), lambda w, tile, exp, *_: (exp[w], 0, 0)),
            pl.BlockSpec(memory_space=pl.ANY),
            pl.BlockSpec((1, 1, d), lambda w, tile, exp, *_: (exp[w], 0, 0)),
        ],
        out_specs=pl.BlockSpec((tr, d), lambda w, tile, *_: (tile[w], 0)),
        scratch_shapes=[pltpu.VMEM((2, d, two_ff), F32), pltpu.VMEM((2, d_ff, d), F32),
                        pltpu.VMEM((d, two_ff), BF16), pltpu.VMEM((d_ff, d), BF16),
                        pltpu.SemaphoreType.DMA((2, 2))],
    )
    return pl.pallas_call(
        _gmm_body,
        grid_spec=grid_spec,
        out_shape=jax.ShapeDtypeStruct((n_rows, d), F32),
        compiler_params=_params(1),
        name="moe_gmm",
    )(*items, xs, w1, b1.reshape(n_exp, 1, two_ff), w2, b2.reshape(n_exp, 1, d))


def _combine_body(pos_ref, x2_ref, gate_ref, gain_ref, ys_ref, yp_ref, ysm_ref, buf_ref, sem, *, n_prompt_tiles):
    tm = x2_ref.shape[0]

    def issue(t, carry):
        for k in range(TOP_K):
            _row_copy(ys_ref, pos_ref[k, t], buf_ref.at[k], t, sem).start()
        return carry

    lax.fori_loop(0, tm, issue, 0)

    def drain(t, carry):
        for k in range(TOP_K):
            _row_copy(ys_ref, pos_ref[k, t], buf_ref.at[k], t, sem).wait()
        return carry

    lax.fori_loop(0, tm, drain, 0)
    gate = gate_ref[...]
    x3 = x2_ref[...]
    for k in range(TOP_K):
        x3 = x3 + gate[:, k:k + 1] * buf_ref[k]
    y = _rms(x3, gain_ref[...])
    is_prompt = pl.program_id(0) < n_prompt_tiles

    @pl.when(is_prompt)
    def _():
        yp_ref[...] = y

    @pl.when(jnp.logical_not(is_prompt))
    def _():
        ysm_ref[...] = y


def _combine(pos, x2, gate_t, gain, ys, n_prompt_rows):
    n, d = x2.shape
    tm = TOKEN_TILE
    npt = n_prompt_rows // tm
    row = lambda i: (i, 0)
    return pl.pallas_call(
        functools.partial(_combine_body, n_prompt_tiles=npt),
        grid=(n // tm,),
        in_specs=[pl.BlockSpec((TOP_K, tm), lambda i: (0, i), memory_space=pltpu.SMEM),
                  pl.BlockSpec((tm, d), row), pl.BlockSpec((tm, LANES), row), _const_spec(gain.shape),
                  pl.BlockSpec(memory_space=pl.ANY)],
        out_specs=[pl.BlockSpec((tm, d), lambda i: (jnp.minimum(i, npt - 1), 0)),
                   pl.BlockSpec((tm, d), lambda i: (jnp.maximum(i - npt, 0), 0))],
        out_shape=[jax.ShapeDtypeStruct((n_prompt_rows, d), F32),
                   jax.ShapeDtypeStruct((n - n_prompt_rows, d), F32)],
        scratch_shapes=[pltpu.VMEM((TOP_K, tm, d), F32), pltpu.SemaphoreType.DMA(())],
        compiler_params=_params(1),
        name="moe_combine",
    )(pos, x2, gate_t, gain, ys)


def _head_group_lanes(per_head, hps):
    groups = per_head.reshape(-1, 1, hps)
    padded = jnp.pad(groups, ((0, 0), (0, 0), (0, LANES - hps)))
    return jnp.broadcast_to(padded, (groups.shape[0], SUBLANES, LANES)).reshape(-1, LANES)


def kernel(x_prompt, x_sample, mem_prompt, state_gdn, state_gdn_conv, state_conf_conv, cache_mem_k, cache_mem_v,
           norm_mix, w_in, gdn_conv_w, gdn_a_log, gdn_dt_bias, gdn_norm, gdn_o, conf_conv_w, conf_conv_b,
           conf_ln_g, conf_ln_b, conf_pw2, conf_pw2_b, w_out, norm_xa, norm_mem, xa_q, xa_kv, xa_o,
           norm_ffn, router_w, router_b, moe_w1, moe_b1, moe_w2, moe_b2, norm_final):
    n_batch, seq, d = x_prompt.shape
    n_seq, n_tok, _ = x_sample.shape
    depth, _, n_heads, dk, dv = state_gdn.shape
    assert depth == 1 and dk == LANES and dv == LANES
    mem_len, xa_heads = cache_mem_k.shape[2], cache_mem_k.shape[3]
    n_experts = router_w.shape[2]
    qkv_w = gdn_conv_w.shape[2]
    v_w = n_heads * dv
    hps = GDN_HEADS_PER_STEP
    n_hg = n_heads // hps
    n_gc = gdn_conv_w.shape[1] - 1
    n_cc = conf_conv_w.shape[1] - 1
    n_p, n_s = n_batch * seq, n_seq * n_tok
    assert seq % TOKEN_TILE == 0 and n_s % TOKEN_TILE == 0 and n_seq % SEQ_BLOCK == 0
    assert n_tok * SEQ_BLOCK <= LANES and n_heads % hps == 0 and 2 * hps <= LANES
    assert n_gc <= min(n_tok, SUBLANES) and n_cc <= CONF_HALO
    row = lambda a: a.reshape(1, -1)
    to_token_major = lambda a: a.reshape(n_seq, n_tok, -1).transpose(1, 0, 2).reshape(n_s, -1)
    to_seq_major = lambda a: a.reshape(n_tok, n_seq, -1).transpose(1, 0, 2).reshape(n_s, -1)

    xp = x_prompt.reshape(n_p, d)
    xs = to_token_major(x_sample)

    w = w_in.reshape(d, -1).astype(BF16)
    ab_lo, ab_hi = qkv_w + v_w, qkv_w + v_w + 2 * n_heads
    w_a = w[:, ab_lo:ab_lo + n_heads].reshape(d, n_hg, hps)
    w_b = w[:, ab_lo + n_heads:ab_hi].reshape(d, n_hg, hps)
    wab = jnp.pad(jnp.concatenate([w_a, w_b], axis=2), ((0, 0), (0, 0), (0, LANES - 2 * hps))).reshape(d, -1)
    qkv, z, ab, cv, sg = _inproj(xp, xs, row(norm_mix), w[:, :ab_lo], wab, w[:, ab_hi:ab_hi + 2 * d],
                                 w[:, ab_hi + 2 * d:], qkv_w)

    alog = _head_group_lanes(gdn_a_log.reshape(-1), hps)
    dtb = _head_group_lanes(gdn_dt_bias.reshape(-1), hps)
    gnorm = row(gdn_norm)
    og_p, s_p, tail_q, tail_k, tail_v = _gdn_prompt(qkv, z, ab, gdn_conv_w, alog, dtb, gnorm,
                                                    n_batch, seq, n_heads, dk)
    og_s, s_s = _gdn_sample(qkv, z, ab, state_gdn_conv, state_gdn, gdn_conv_w, alog, dtb, gnorm,
                            n_p, n_seq, n_tok, n_heads, dk)

    conf_args = (conf_conv_w, row(conf_conv_b), row(conf_ln_g), row(conf_ln_b))
    yb_p, cconv_tail = _conf_prompt(cv, *conf_args, n_batch, seq)
    yb_s, cconv_s = _conf_sample(cv, state_conf_conv, *conf_args, n_p, n_seq, n_tok)

    bf = lambda a: a.reshape(a.shape[1:]).astype(BF16)
    x1, q = _merge(og_p, og_s, yb_p, yb_s, sg, xp, xs, bf(gdn_o), bf(conf_pw2), row(conf_pw2_b), bf(w_out),
                   row(norm_xa), bf(xa_q))

    mk_p, mv_p = _memkv(mem_prompt.reshape(n_batch * mem_len, d), row(norm_mem), bf(xa_kv))
    o_p = _attn_prompt(q, mk_p, mv_p, n_batch, seq, mem_len, xa_heads)
    o_s = to_token_major(_attn_sample(to_seq_major(q[n_p:]), cache_mem_k, cache_mem_v, n_tok))

    x2, hf, idx, gate_t = _router(o_p, o_s, x1, bf(xa_o), row(norm_ffn), router_w.reshape(d, n_experts).T,
                                  router_b.reshape(-1, 1))
    pos, counts = _positions(idx, n_experts)
    xsorted = _dispatch(pos, hf)
    ysorted = _gmm(xsorted, counts[:, 0], moe_w1, moe_b1, moe_w2, moe_b2)
    y_p, y_s = _combine(pos, x2, gate_t, row(norm_final), ysorted, n_p)

    y_prompt = y_p.reshape(n_batch, seq, d)
    y_sample = to_seq_major(y_s).reshape(n_seq, n_tok, d)
    gconv_p = jnp.concatenate([tail_q, tail_k, tail_v], axis=2)[:, SUBLANES - n_gc:, :]
    gconv_s = to_seq_major(qkv[n_p:]).reshape(n_seq, n_tok, qkv_w)[:, n_tok - n_gc:, :].astype(F32)
    cconv_p = cconv_tail[:, CONF_HALO - n_cc:, :]
    kv_shape = (1, n_batch, mem_len, xa_heads, d // xa_heads)
    return (y_prompt, y_sample, s_p[None], gconv_p[None], cconv_p[None], mk_p.reshape(kv_shape),
            mv_p.reshape(kv_shape), s_s[None], gconv_s[None], cconv_s[None])
```

```python
import functools

import jax
import jax.numpy as jnp
from jax import lax
from jax.experimental import pallas as pl
from jax.experimental.pallas import tpu as pltpu

F32, BF16, I32 = jnp.float32, jnp.bfloat16, jnp.int32
HIGHEST = lax.Precision.HIGHEST

LANES = 128
SUBLANES = 8
V7X_VMEM_REQUEST = 56 * 1024 * 1024

TOKEN_TILE = 512
GDN_CHUNK = 64
GDN_GROUP = 256
GDN_HEADS_PER_STEP = 8
SEQ_BLOCK = 16
CONF_HALO = 32
CONF_ROWS = 32
CONF_LANE_SPLIT = 8
CONF_NORM_ROWS = 128
MOE_ROW_TILE = 256
CUMSUM_CHUNK = 512

TOP_K = 4
SWIGLU_ALPHA = 1.702
SWIGLU_LIMIT = 7.0
RMS_EPS = 1e-6
LN_EPS = 1e-5
L2_EPS = 1e-6


def _params(n_axes):
    return pltpu.CompilerParams(dimension_semantics=("arbitrary",) * n_axes, vmem_limit_bytes=V7X_VMEM_REQUEST)


def _const_spec(shape):
    zeros = (0,) * len(shape)
    return pl.BlockSpec(shape, lambda *_: zeros, pipeline_mode=pl.Buffered(1))


def _layer_spec(shape):
    zeros = (0,) * len(shape)
    return pl.BlockSpec((None,) + tuple(shape[1:]), lambda *_: zeros, pipeline_mode=pl.Buffered(1))


def _mm(a, b):
    return jnp.dot(a.astype(BF16), b.astype(BF16), preferred_element_type=F32)


def _mm_nt(a, b):
    return lax.dot_general(a.astype(BF16), b.astype(BF16), (((1,), (1,)), ((), ())), preferred_element_type=F32)


def _rms(x, gain):
    return x * lax.rsqrt(jnp.mean(x * x, axis=-1, keepdims=True) + RMS_EPS) * gain


def _silu(x):
    return x * jax.nn.sigmoid(x)


def _softplus(x):
    return jnp.maximum(x, 0.0) + jnp.log1p(jnp.exp(-jnp.abs(x)))


def _split_hi_lo(x):
    hi = x.astype(BF16)
    lo = (x - hi.astype(F32)).astype(BF16)
    return hi, lo


def _inproj_body(xp_ref, xs_ref, gain_ref, wqkvz_ref, wab_ref, wglu_ref, wgate_ref,
                 qkv_ref, z_ref, ab_ref, cv_ref, sg_ref, *, n_prompt_tiles):
    i = pl.program_id(0)
    d = xp_ref.shape[1]
    x = jnp.where(i < n_prompt_tiles, xp_ref[...], xs_ref[...])
    h = _rms(x, gain_ref[...]).astype(BF16)
    qkv_w = qkv_ref.shape[1]
    for c in range(0, qkv_w, d):
        qkv_ref[:, c:c + d] = jnp.dot(h, wqkvz_ref[:, c:c + d], preferred_element_type=F32).astype(BF16)
    z_ref[...] = jnp.dot(h, wqkvz_ref[:, qkv_w:], preferred_element_type=F32).astype(BF16)
    ab_ref[...] = jnp.dot(h, wab_ref[...], preferred_element_type=F32)
    glu_a = jnp.dot(h, wglu_ref[:, :d], preferred_element_type=F32)
    glu_b = jnp.dot(h, wglu_ref[:, d:], preferred_element_type=F32)
    cv_ref[...] = glu_a * jax.nn.sigmoid(glu_b)
    for c in range(0, 2 * d, d):
        sg_ref[:, c:c + d] = jax.nn.sigmoid(
            jnp.dot(h, wgate_ref[:, c:c + d], preferred_element_type=F32)).astype(BF16)


def _inproj(xp, xs, gain, wqkvz, wab, wglu, wgate, qkv_w):
    n_p, d = xp.shape
    n_s = xs.shape[0]
    tm = TOKEN_TILE
    npt, nst = n_p // tm, n_s // tm
    n = n_p + n_s
    ab_w = wab.shape[1]
    row = lambda i: (i, 0)
    return pl.pallas_call(
        functools.partial(_inproj_body, n_prompt_tiles=npt),
        grid=(npt + nst,),
        in_specs=[
            pl.BlockSpec((tm, d), lambda i: (jnp.minimum(i, npt - 1), 0)),
            pl.BlockSpec((tm, d), lambda i: (jnp.maximum(i - npt, 0), 0)),
            _const_spec(gain.shape), _const_spec(wqkvz.shape), _const_spec(wab.shape),
            _const_spec(wglu.shape), _const_spec(wgate.shape),
        ],
        out_specs=[
            pl.BlockSpec((tm, qkv_w), row), pl.BlockSpec((tm, d), row), pl.BlockSpec((tm, ab_w), row),
            pl.BlockSpec((tm, d), row), pl.BlockSpec((tm, 2 * d), row),
        ],
        out_shape=[
            jax.ShapeDtypeStruct((n, qkv_w), BF16), jax.ShapeDtypeStruct((n, d), BF16),
            jax.ShapeDtypeStruct((n, ab_w), F32), jax.ShapeDtypeStruct((n, d), F32),
            jax.ShapeDtypeStruct((n, 2 * d), BF16),
        ],
        compiler_params=_params(1),
        name="inproj",
    )(xp, xs, gain, wqkvz, wab, wglu, wgate)


def _decay_scalars(ab, alog_row, dtb_row):
    return -jnp.exp(alog_row) * _softplus(ab + dtb_row), jax.nn.sigmoid(ab)


def _cumsum_rows(incl_bf16, g):
    hi, lo = _split_hi_lo(g)
    both = jnp.dot(incl_bf16, jnp.concatenate([hi, lo], axis=1), preferred_element_type=F32)
    return both[:, :LANES] + both[:, LANES:]


def _l2norm(x):
    return x * lax.rsqrt(jnp.sum(x * x, axis=-1, keepdims=True) + L2_EPS)


def _pair_masks(r, same):
    ri = lax.broadcasted_iota(I32, (r, r), 0)
    ci = lax.broadcasted_iota(I32, (r, r), 1)
    pair = same(ri, ci)
    as_f = lambda m: jnp.where(m, 1.0, 0.0).astype(F32)
    return as_f(ri == ci), as_f(pair & (ri >= ci)), -as_f(pair & (ri > ci))


def _lane_col(x, lane):
    return jnp.broadcast_to(x[:, lane:lane + 1], (x.shape[0], LANES))


def _wy_heads(q, k, v, g_all, beta_all, hps, eye, incl, neg_strict, n_doublings):
    r = q[0].shape[0]
    heads = range(len(q))
    gc_all = _cumsum_rows(incl.astype(BF16), g_all)
    pad = jnp.zeros((max(LANES - r, 0), LANES), F32)
    gc_t = jnp.transpose(jnp.concatenate([gc_all, pad], axis=0) if r < LANES else gc_all)
    gc = [_lane_col(gc_all, h) for h in heads]
    beta = [_lane_col(beta_all, hps + h) for h in heads]
    pair_decay = []
    for h in heads:
        gc_wide = jnp.concatenate([gc[h]] * (r // LANES), axis=1) if r > LANES else gc[h][:, :r]
        pair_decay.append(jnp.exp(jnp.minimum(gc_wide - gc_t[h:h + 1, :r], 0.0)))
    kb = [k[h] * beta[h] for h in heads]
    gram = [_mm_nt(jnp.concatenate([kb[h], q[h]], axis=0), k[h]) for h in heads]
    neg_m = [gram[h][:r] * (pair_decay[h] * neg_strict) for h in heads]
    p = [gram[h][r:] * (pair_decay[h] * incl) for h in heads]
    t_inv = [eye + neg_m[h] for h in heads]
    power = neg_m
    for _ in range(n_doublings):
        power = [_mm(power[h], power[h]) for h in heads]
        t_inv = [t_inv[h] + _mm(t_inv[h], power[h]) for h in heads]
    sol = [_mm(t_inv[h], jnp.concatenate([v[h] * beta[h], kb[h] * jnp.exp(gc[h])], axis=1)) for h in heads]
    return gc, [s[:, :LANES] for s in sol], [s[:, LANES:] for s in sol], p


def _gated_out(o, gnorm, z):
    on = o * lax.rsqrt(jnp.mean(o * o, axis=-1, keepdims=True) + RMS_EPS) * gnorm
    return (on * _silu(z)).astype(BF16)


def _gdn_prompt_body(q_ref, k_ref, v_ref, z_ref, ab_ref, cwq_ref, cwk_ref, cwv_ref, alog_ref, dtb_ref,
                     gnorm_ref, og_ref, sout_ref, tq_ref, tk_ref, tv_ref, halo_ref, s_ref, mask_ref, *, n_taps):
    hps, dk = s_ref.shape[0], s_ref.shape[1]
    grp, chunk = GDN_GROUP, GDN_CHUNK
    n_groups = q_ref.shape[0] // grp
    halo_rows = halo_ref.shape[1]
    halo_ref[...] = jnp.zeros(halo_ref.shape, F32)
    s_ref[...] = jnp.zeros(s_ref.shape, F32)
    for i, m in enumerate(_pair_masks(grp, lambda ri, ci: (ri // chunk) == (ci // chunk))):
        mask_ref[i] = m
    heads = range(hps)

    def group(gi, carry):
        rows = pl.ds(pl.multiple_of(gi * grp, grp), grp)
        conv = []
        for part, (x_ref, cw_ref) in enumerate(((q_ref, cwq_ref), (k_ref, cwk_ref), (v_ref, cwv_ref))):
            xg = x_ref[rows, :].astype(F32)
            xw = jnp.concatenate([halo_ref[part], xg], axis=0)
            halo_ref[part] = xg[grp - halo_rows:, :]
            y = None
            for j in range(n_taps):
                off = halo_rows - (n_taps - 1) + j
                term = cw_ref[j:j + 1, :] * xw[off:off + grp, :]
                y = term if y is None else y + term
            conv.append(_silu(y))
        sl = [slice(h * dk, (h + 1) * dk) for h in heads]
        q = [_l2norm(conv[0][:, sl[h]]) * (dk ** -0.5) for h in heads]
        k = [_l2norm(conv[1][:, sl[h]]) for h in heads]
        v = [conv[2][:, sl[h]] for h in heads]
        g_all, beta_all = _decay_scalars(ab_ref[rows, :], alog_ref[0:1, :], dtb_ref[0:1, :])
        gc, u, w, p = _wy_heads(q, k, v, g_all, beta_all, hps, mask_ref[0], mask_ref[1], mask_ref[2], 5)
        qe = [q[h] * jnp.exp(gc[h]) for h in heads]
        state = [s_ref[h] for h in heads]
        outs = [[] for _ in heads]
        for j in range(grp // chunk):
            cs = slice(j * chunk, (j + 1) * chunk)
            g_last = [gc[h][(j + 1) * chunk - 1:(j + 1) * chunk, :] for h in heads]
            k_dec = [k[h][cs] * jnp.exp(g_last[h] - gc[h][cs]) for h in heads]
            ws_qs = [_mm(jnp.concatenate([w[h][cs], qe[h][cs]], axis=0), state[h]) for h in heads]
            v_new = [u[h][cs] - ws_qs[h][:chunk] for h in heads]
            for h in heads:
                outs[h].append(ws_qs[h][chunk:] + _mm(p[h][cs, cs], v_new[h]))
            state = [state[h] * jnp.exp(g_last[h]) + lax.dot_general(
                k_dec[h].astype(BF16), v_new[h].astype(BF16), (((0,), (0,)), ((), ())),
                preferred_element_type=F32) for h in heads]
        for h in heads:
            s_ref[h] = state[h]
            o = jnp.concatenate(outs[h], axis=0)
            og_ref[rows, sl[h]] = _gated_out(o, gnorm_ref[...], z_ref[rows, sl[h]].astype(F32))
        return carry

    lax.fori_loop(0, n_groups, group, 0)
    sout_ref[0] = s_ref[...]
    tq_ref[0] = halo_ref[0]
    tk_ref[0] = halo_ref[1]
    tv_ref[0] = halo_ref[2]


def _gdn_prompt(qkv, z, ab, conv_w, alog, dtb, gnorm, n_batch, seq, n_heads, dk):
    hps = GDN_HEADS_PER_STEP
    hgw = hps * dk
    n_hg = n_heads // hps
    n_taps = conv_w.shape[1]
    grp = GDN_GROUP
    col = lambda part: (lambda b, h: (b, part * n_hg + h))
    cwcol = lambda part: (lambda b, h: (0, 0, part * n_hg + h))
    bh = lambda b, h: (b, h)
    tail = jax.ShapeDtypeStruct((n_batch, SUBLANES, n_heads * dk), F32)
    tail_spec = pl.BlockSpec((1, SUBLANES, hgw), lambda b, h: (b, 0, h))
    return pl.pallas_call(
        functools.partial(_gdn_prompt_body, n_taps=n_taps),
        grid=(n_batch, n_hg),
        in_specs=[
            pl.BlockSpec((seq, hgw), col(0)), pl.BlockSpec((seq, hgw), col(1)), pl.BlockSpec((seq, hgw), col(2)),
            pl.BlockSpec((seq, hgw), bh),
            pl.BlockSpec((seq, LANES), bh),
            pl.BlockSpec((None, n_taps, hgw), cwcol(0)), pl.BlockSpec((None, n_taps, hgw), cwcol(1)),
            pl.BlockSpec((None, n_taps, hgw), cwcol(2)),
            pl.BlockSpec((SUBLANES, LANES), lambda b, h: (h, 0)),
            pl.BlockSpec((SUBLANES, LANES), lambda b, h: (h, 0)),
            _const_spec(gnorm.shape),
        ],
        out_specs=[
            pl.BlockSpec((seq, hgw), bh),
            pl.BlockSpec((1, hps, dk, dk), lambda b, h: (b, h, 0, 0)),
            tail_spec, tail_spec, tail_spec,
        ],
        out_shape=[
            jax.ShapeDtypeStruct((n_batch * seq, n_heads * dk), BF16),
            jax.ShapeDtypeStruct((n_batch, n_heads, dk, dk), F32),
            tail, tail, tail,
        ],
        scratch_shapes=[pltpu.VMEM((3, SUBLANES, hgw), F32), pltpu.VMEM((hps, dk, dk), F32),
                        pltpu.VMEM((3, grp, grp), F32)],
        compiler_params=_params(2),
        name="gdn_prompt",
    )(qkv, qkv, qkv, z, ab, conv_w, conv_w, conv_w, alog, dtb, gnorm)


def _gdn_sample_body(*refs, n_tok, n_taps):
    sb = SEQ_BLOCK
    it = iter(refs)
    take = lambda n: [next(it) for _ in range(n)]
    x_refs = [take(n_tok) for _ in range(3)]
    cb_refs = take(3)
    z_refs = take(n_tok)
    ab_refs = take(n_tok)
    s0_ref = next(it)
    cw_refs = take(3)
    alog_ref, dtb_ref, gnorm_ref = take(3)
    og_refs = take(n_tok)
    sout_ref = next(it)

    hps, dk = s0_ref.shape[1], s0_ref.shape[2]
    heads = range(hps)
    r = n_tok * sb

    conv = []
    for part in range(3):
        ext = [cb_refs[part][:, j, :] for j in range(n_taps - 1)]
        ext += [x_refs[part][t][...].astype(F32) for t in range(n_tok)]
        ys = []
        for t in range(n_tok):
            y = None
            for j in range(n_taps):
                term = cw_refs[part][j:j + 1, :] * ext[t + j]
                y = term if y is None else y + term
            ys.append(_silu(y))
        conv.append(jnp.concatenate(ys, axis=0))
    ab = jnp.concatenate([a[...] for a in ab_refs], axis=0)

    eye, incl, neg_strict = _pair_masks(r, lambda ri, ci: (ri % sb) == (ci % sb))
    row_seq = lax.broadcasted_iota(I32, (2 * r, dk), 0) % sb
    col_seq = lax.broadcasted_iota(I32, (dk, LANES), 1) % sb
    n_doublings = max((n_tok - 1).bit_length() - 1, 0)

    sl = [slice(h * dk, (h + 1) * dk) for h in heads]
    q = [_l2norm(conv[0][:, sl[h]]) * (dk ** -0.5) for h in heads]
    k = [_l2norm(conv[1][:, sl[h]]) for h in heads]
    v = [conv[2][:, sl[h]] for h in heads]
    g_all, beta_all = _decay_scalars(ab, alog_ref[0:1, :], dtb_ref[0:1, :])
    gc, u, w, p = _wy_heads(q, k, v, g_all, beta_all, hps, eye, incl, neg_strict, n_doublings)
    for h in heads:
        wq = jnp.concatenate([w[h], q[h] * jnp.exp(gc[h])], axis=0).astype(BF16)
        ws_qs = jnp.zeros((2 * r, dk), F32)
        for s in range(sb):
            ws_qs = jnp.where(row_seq == s, _mm(wq, s0_ref[s, h]), ws_qs)
        v_new = u[h] - ws_qs[:r]
        o = ws_qs[r:] + _mm(p[h], v_new)
        z = jnp.concatenate([zr[:, sl[h]] for zr in z_refs], axis=0).astype(F32)
        og = _gated_out(o, gnorm_ref[...], z)
        for t in range(n_tok):
            og_refs[t][:, sl[h]] = og[t * sb:(t + 1) * sb]
        g_end = gc[h][r - sb:, :]
        k_dec = k[h] * jnp.exp(jnp.concatenate([g_end] * n_tok, axis=0) - gc[h])
        k_dec_t = jnp.transpose(jnp.concatenate([k_dec, jnp.zeros((LANES - r, dk), F32)], axis=0))
        v_new_pad = jnp.concatenate([v_new, jnp.zeros((LANES - r, dk), F32)], axis=0).astype(BF16)
        for s in range(sb):
            upd = _mm(jnp.where(col_seq == s, k_dec_t, 0.0), v_new_pad)
            sout_ref[s, h] = s0_ref[s, h] * jnp.exp(g_end[s:s + 1, :]) + upd


def _gdn_sample(qkv, z, ab, conv_buf, s0, conv_w, alog, dtb, gnorm, n_prompt_rows, n_seq, n_tok, n_heads, dk):
    sb, hps = SEQ_BLOCK, GDN_HEADS_PER_STEP
    hgw = hps * dk
    n_hg = n_heads // hps
    n_taps = conv_w.shape[1]
    base = n_prompt_rows // sb
    per_t = n_seq // sb

    def tok_rows(t, colfn):
        return lambda i, h: (base + t * per_t + i, colfn(h))

    in_specs, args = [], []
    for part in range(3):
        for t in range(n_tok):
            in_specs.append(pl.BlockSpec((sb, hgw), tok_rows(t, lambda h, part=part: part * n_hg + h)))
            args.append(qkv)
    for part in range(3):
        in_specs.append(pl.BlockSpec((None, sb, n_taps - 1, hgw),
                                     lambda i, h, part=part: (0, i, 0, part * n_hg + h)))
        args.append(conv_buf)
    for t in range(n_tok):
        in_specs.append(pl.BlockSpec((sb, hgw), tok_rows(t, lambda h: h)))
        args.append(z)
    for t in range(n_tok):
        in_specs.append(pl.BlockSpec((sb, LANES), tok_rows(t, lambda h: h)))
        args.append(ab)
    in_specs.append(pl.BlockSpec((None, sb, hps, dk, dk), lambda i, h: (0, i, h, 0, 0)))
    args.append(s0)
    for part in range(3):
        in_specs.append(pl.BlockSpec((None, n_taps, hgw), lambda i, h, part=part: (0, 0, part * n_hg + h)))
        args.append(conv_w)
    in_specs += [pl.BlockSpec((SUBLANES, LANES), lambda i, h: (h, 0)),
                 pl.BlockSpec((SUBLANES, LANES), lambda i, h: (h, 0)), _const_spec(gnorm.shape)]
    args += [alog, dtb, gnorm]
    outs = pl.pallas_call(
        functools.partial(_gdn_sample_body, n_tok=n_tok, n_taps=n_taps),
        grid=(per_t, n_hg),
        in_specs=in_specs,
        out_specs=[pl.BlockSpec((sb, hgw), lambda i, h: (i, h))] * n_tok
        + [pl.BlockSpec((sb, hps, dk, dk), lambda i, h: (i, h, 0, 0))],
        out_shape=[jax.ShapeDtypeStruct((n_seq, n_heads * dk), BF16)] * n_tok
        + [jax.ShapeDtypeStruct(s0.shape[1:], F32)],
        compiler_params=_params(2),
        name="gdn_sample",
    )(*args)
    return jnp.concatenate(outs[:n_tok], axis=0), outs[n_tok]


def _ln_silu(y, lg, lb):
    mu = jnp.mean(y, axis=-1, keepdims=True)
    yc = y - mu
    var = jnp.mean(yc * yc, axis=-1, keepdims=True)
    return _silu(yc * lax.rsqrt(var + LN_EPS) * lg + lb).astype(BF16)


def _conf_prompt_body(cv_ref, halo_ref, cw_ref, cb_ref, lg_ref, lb_ref, out_ref, tail_ref, rot_ref, conv_ref,
                      *, n_taps):
    tile_in_seq = pl.program_id(1)
    tt, c = cv_ref.shape
    halo = CONF_HALO
    rot_ref[0, 0:halo, :] = jnp.where(tile_in_seq == 0, 0.0, halo_ref[...])
    rot_ref[0, halo:halo + tt, :] = cv_ref[...]
    rot_ref[0, halo + tt:halo + tt + SUBLANES, :] = jnp.zeros((SUBLANES, c), F32)
    step = 32

    def rotate(i, carry):
        base = pl.multiple_of(i * step, step)
        win = rot_ref[0, pl.ds(base, step + SUBLANES), :]
        for r in range(1, SUBLANES):
            rot_ref[r, pl.ds(base, step), :] = win[r:r + step, :]
        return carry

    lax.fori_loop(0, (halo + tt) // step, rotate, 0)
    lead = halo - (n_taps - 1)
    rows = CONF_ROWS
    groups = rows // SUBLANES
    lane_w = c // CONF_LANE_SPLIT
    by_rot = [[(j, (lead + j) // SUBLANES) for j in range(n_taps) if (lead + j) % SUBLANES == r]
              for r in range(SUBLANES)]

    for lh in range(CONF_LANE_SPLIT):
        ls = slice(lh * lane_w, (lh + 1) * lane_w)

        def conv(i, carry, ls=ls):
            base = pl.multiple_of(i * rows, rows)
            accs = [jnp.broadcast_to(cb_ref[:, ls], (SUBLANES, lane_w))] * groups
            for r, taps in enumerate(by_rot):
                if not taps:
                    continue
                lo = min(a for _, a in taps)
                hi = max(a for _, a in taps) + groups
                slabs = {m: rot_ref[r, pl.ds(pl.multiple_of(base + m * SUBLANES, SUBLANES), SUBLANES), ls]
                         for m in range(lo, hi)}
                for j, a in taps:
                    w8 = cw_ref[j, :, ls]
                    accs = [acc + w8 * slabs[a + g] for g, acc in enumerate(accs)]
            conv_ref[pl.ds(base, rows), ls] = jnp.concatenate(accs, axis=0)
            return carry

        lax.fori_loop(0, tt // rows, conv, 0)

    nrows = CONF_NORM_ROWS

    def norm(i, carry):
        base = pl.multiple_of(i * nrows, nrows)
        out_ref[pl.ds(base, nrows), :] = _ln_silu(conv_ref[pl.ds(base, nrows), :], lg_ref[...], lb_ref[...])
        return carry

    lax.fori_loop(0, tt // nrows, norm, 0)

    @pl.when(tile_in_seq == pl.num_programs(1) - 1)
    def _():
        tail_ref[0] = cv_ref[tt - halo:, :]


def _conf_prompt(cv, cw, cb, lg, lb, n_batch, seq):
    c = cv.shape[1]
    tt = TOKEN_TILE
    tiles = seq // tt
    halo_per_tile = tt // CONF_HALO
    n_taps = cw.shape[1]
    cw8 = jnp.broadcast_to(cw.reshape(n_taps, 1, c), (n_taps, SUBLANES, c))
    return pl.pallas_call(
        functools.partial(_conf_prompt_body, n_taps=n_taps),
        grid=(n_batch, tiles),
        in_specs=[
            pl.BlockSpec((tt, c), lambda b, j: (b * tiles + j, 0)),
            pl.BlockSpec((CONF_HALO, c), lambda b, j: (jnp.maximum((b * tiles + j) * halo_per_tile - 1, 0), 0)),
            _const_spec(cw8.shape), _const_spec(cb.shape), _const_spec(lg.shape), _const_spec(lb.shape),
        ],
        out_specs=[pl.BlockSpec((tt, c), lambda b, j: (b * tiles + j, 0)),
                   pl.BlockSpec((1, CONF_HALO, c), lambda b, j: (b, 0, 0))],
        out_shape=[jax.ShapeDtypeStruct((n_batch * seq, c), BF16),
                   jax.ShapeDtypeStruct((n_batch, CONF_HALO, c), F32)],
        scratch_shapes=[pltpu.VMEM((SUBLANES, CONF_HALO + tt + SUBLANES, c), F32), pltpu.VMEM((tt, c), F32)],
        compiler_params=_params(2),
        name="conf_prompt",
    )(cv, cv, cw8, cb, lg, lb)


def _conf_sample_body(*refs, n_tok, n_taps):
    buf_ref = refs[0]
    cv_refs = refs[1:1 + n_tok]
    cw_ref, cb_ref, lg_ref, lb_ref = refs[1 + n_tok:5 + n_tok]
    out_refs = refs[5 + n_tok:5 + 2 * n_tok]
    newbuf_ref = refs[5 + 2 * n_tok]
    n_hist = n_taps - 1

    def ext(i):
        return buf_ref[:, i, :] if i < n_hist else cv_refs[i - n_hist][...]

    for t in range(n_tok):
        acc = jnp.broadcast_to(cb_ref[...], cv_refs[0].shape)
        for j in range(n_taps):
            acc = acc + cw_ref[j:j + 1, :] * ext(t + j)
        out_refs[t][...] = _ln_silu(acc, lg_ref[...], lb_ref[...])
    for i in range(n_hist):
        newbuf_ref[:, i, :] = ext(i + n_tok)


def _conf_sample(cv, buf, cw, cb, lg, lb, n_prompt_rows, n_seq, n_tok):
    sb = SEQ_BLOCK
    c = cv.shape[1]
    n_taps = cw.shape[1]
    base, per_t = n_prompt_rows // sb, n_seq // sb
    in_specs = [pl.BlockSpec((None, sb, n_taps - 1, c), lambda i: (0, i, 0, 0))]
    in_specs += [pl.BlockSpec((sb, c), lambda i, t=t: (base + t * per_t + i, 0)) for t in range(n_tok)]
    in_specs += [_layer_spec(cw.shape), _const_spec(cb.shape), _const_spec(lg.shape), _const_spec(lb.shape)]
    outs = pl.pallas_call(
        functools.partial(_conf_sample_body, n_tok=n_tok, n_taps=n_taps),
        grid=(per_t,),
        in_specs=in_specs,
        out_specs=[pl.BlockSpec((sb, c), lambda i: (i, 0))] * n_tok
        + [pl.BlockSpec((sb, n_taps - 1, c), lambda i: (i, 0, 0))],
        out_shape=[jax.ShapeDtypeStruct((n_seq, c), BF16)] * n_tok
        + [jax.ShapeDtypeStruct((n_seq, n_taps - 1, c), F32)],
        compiler_params=_params(1),
        name="conf_sample",
    )(buf, *([cv] * n_tok), cw, cb, lg, lb)
    return jnp.concatenate(outs[:n_tok], axis=0), outs[n_tok]


def _merge_body(ogp_ref, ogs_ref, ybp_ref, ybs_ref, sg_ref, xp_ref, xs_ref, wgo_ref, wpw_ref, bpw_ref,
                wout_ref, gxa_ref, wq_ref, x1_ref, q_ref, *, n_prompt_tiles):
    is_prompt = pl.program_id(0) < n_prompt_tiles
    d = xp_ref.shape[1]
    og = jnp.where(is_prompt, ogp_ref[...], ogs_ref[...])
    yb_in = jnp.where(is_prompt, ybp_ref[...], ybs_ref[...])
    x = jnp.where(is_prompt, xp_ref[...], xs_ref[...])
    y_a = jnp.dot(og, wgo_ref[...], preferred_element_type=F32)
    y_b = jnp.dot(yb_in, wpw_ref[...], preferred_element_type=F32) + bpw_ref[...]
    mixed = sg_ref[:, :d].astype(F32) * y_a + sg_ref[:, d:].astype(F32) * y_b
    x1 = x + jnp.dot(mixed.astype(BF16), wout_ref[...], preferred_element_type=F32)
    x1_ref[...] = x1
    q_ref[...] = jnp.dot(_rms(x1, gxa_ref[...]).astype(BF16), wq_ref[...],
                         preferred_element_type=F32).astype(BF16)


def _dual(tm, cols, npt):
    return [pl.BlockSpec((tm, cols), lambda i: (jnp.minimum(i, npt - 1), 0)),
            pl.BlockSpec((tm, cols), lambda i: (jnp.maximum(i - npt, 0), 0))]


def _merge(og_p, og_s, yb_p, yb_s, sg, xp, xs, wgo, wpw, bpw, wout, gxa, wq):
    n_p, d = xp.shape
    tm = TOKEN_TILE
    npt, nst = n_p // tm, xs.shape[0] // tm
    n = n_p + xs.shape[0]
    row = lambda i: (i, 0)
    return pl.pallas_call(
        functools.partial(_merge_body, n_prompt_tiles=npt),
        grid=(npt + nst,),
        in_specs=_dual(tm, d, npt) + _dual(tm, d, npt) + [pl.BlockSpec((tm, 2 * d), row)] + _dual(tm, d, npt)
        + [_const_spec(a.shape) for a in (wgo, wpw, bpw, wout, gxa, wq)],
        out_specs=[pl.BlockSpec((tm, d), row), pl.BlockSpec((tm, d), row)],
        out_shape=[jax.ShapeDtypeStruct((n, d), F32), jax.ShapeDtypeStruct((n, d), BF16)],
        compiler_params=_params(1),
        name="merge",
    )(og_p, og_s, yb_p, yb_s, sg, xp, xs, wgo, wpw, bpw, wout, gxa, wq)


def _memkv_body(m_ref, gain_ref, w_ref, k_ref, v_ref):
    d = k_ref.shape[1]
    h = _rms(m_ref[...], gain_ref[...]).astype(BF16)
    k_ref[...] = jnp.dot(h, w_ref[:, :d], preferred_element_type=F32)
    v_ref[...] = jnp.dot(h, w_ref[:, d:], preferred_element_type=F32)


def _memkv(mem, gain, w):
    n, d = mem.shape
    tm = min(TOKEN_TILE, n)
    row = lambda i: (i, 0)
    return pl.pallas_call(
        _memkv_body,
        grid=(n // tm,),
        in_specs=[pl.BlockSpec((tm, d), row), _const_spec(gain.shape), _const_spec(w.shape)],
        out_specs=[pl.BlockSpec((tm, d), row)] * 2,
        out_shape=[jax.ShapeDtypeStruct((n, d), F32)] * 2,
        compiler_params=_params(1),
        name="memkv",
    )(mem, gain, w)


def _softmax_rows(s):
    e = jnp.exp(s - jnp.max(s, axis=-1, keepdims=True))
    return e / jnp.sum(e, axis=-1, keepdims=True)


def _attn_prompt_body(q_ref, k_ref, v_ref, o_ref, *, n_heads):
    dh = q_ref.shape[1] // n_heads
    for h in range(n_heads):
        sl = slice(h * dh, (h + 1) * dh)
        s = _mm_nt(q_ref[:, sl], k_ref[:, sl]) * (dh ** -0.5)
        o_ref[:, sl] = _mm(_softmax_rows(s), v_ref[:, sl]).astype(BF16)


def _attn_prompt(q, k, v, n_batch, seq, mem_len, n_heads):
    d = q.shape[1]
    tq = TOKEN_TILE
    tiles = seq // tq
    return pl.pallas_call(
        functools.partial(_attn_prompt_body, n_heads=n_heads),
        grid=(n_batch, tiles),
        in_specs=[
            pl.BlockSpec((tq, d), lambda b, j: (b * tiles + j, 0)),
            pl.BlockSpec((mem_len, d), lambda b, j: (b, 0)),
            pl.BlockSpec((mem_len, d), lambda b, j: (b, 0)),
        ],
        out_specs=pl.BlockSpec((tq, d), lambda b, j: (b * tiles + j, 0)),
        out_shape=jax.ShapeDtypeStruct((n_batch * seq, d), BF16),
        compiler_params=_params(2),
        name="attn_prompt",
    )(q, k, v)


def _attn_sample_body(q_ref, k_hbm, v_hbm, o_ref, kbuf, vbuf, sem):
    _, sb, mem_len, dh = kbuf.shape
    n_heads = k_hbm.shape[3]
    rows_per_seq = q_ref.shape[0] // sb
    blk = pl.program_id(0)

    def head_copies(block, h):
        seqs = pl.ds(block * sb, sb)
        return (pltpu.make_async_copy(k_hbm.at[0, seqs, :, h, :], kbuf.at[h % 2], sem.at[0, h % 2]),
                pltpu.make_async_copy(v_hbm.at[0, seqs, :, h, :], vbuf.at[h % 2], sem.at[1, h % 2]))

    def start(block, h):
        for cp in head_copies(block, h):
            cp.start()

    @pl.when(blk == 0)
    def _():
        start(blk, 0)

    for h in range(n_heads):
        if h + 1 < n_heads:
            start(blk, h + 1)
        else:
            @pl.when(blk + 1 < pl.num_programs(0))
            def _():
                start(blk + 1, 0)
        for cp in head_copies(blk, h):
            cp.wait()
        sl = slice(h * dh, (h + 1) * dh)
        keys = kbuf[h % 2].reshape(sb * mem_len, dh)
        vals = vbuf[h % 2].reshape(sb * mem_len, dh)
        s = _mm_nt(q_ref[:, sl], keys) * (dh ** -0.5)
        row_seq = lax.broadcasted_iota(I32, s.shape, 0) // rows_per_seq
        col_seq = lax.broadcasted_iota(I32, s.shape, 1) // mem_len
        p = _softmax_rows(jnp.where(row_seq == col_seq, s, -jnp.inf))
        o_ref[:, sl] = _mm(p, vals).astype(BF16)


def _attn_sample(q_seq_major, kc, vc, n_tok):
    n_s, d = q_seq_major.shape
    _, _, mem_len, n_heads, dh = kc.shape
    assert n_heads % 2 == 0
    sb = SEQ_BLOCK
    rows = sb * n_tok
    return pl.pallas_call(
        _attn_sample_body,
        grid=(n_s // rows,),
        in_specs=[pl.BlockSpec((rows, d), lambda i: (i, 0)), pl.BlockSpec(memory_space=pl.ANY),
                  pl.BlockSpec(memory_space=pl.ANY)],
        out_specs=pl.BlockSpec((rows, d), lambda i: (i, 0)),
        out_shape=jax.ShapeDtypeStruct((n_s, d), BF16),
        scratch_shapes=[pltpu.VMEM((2, sb, mem_len, dh), F32), pltpu.VMEM((2, sb, mem_len, dh), F32),
                        pltpu.SemaphoreType.DMA((2, 2))],
        compiler_params=_params(1),
        name="attn_sample",
    )(q_seq_major, kc, vc)


def _router_body(op_ref, os_ref, x1_ref, wo_ref, gain_ref, rwt_ref, rb_ref,
                 x2_ref, hf_ref, idx_ref, gate_ref, *, n_prompt_tiles):
    o = jnp.where(pl.program_id(0) < n_prompt_tiles, op_ref[...], os_ref[...])
    x2 = x1_ref[...] + jnp.dot(o, wo_ref[...], preferred_element_type=F32)
    x2_ref[...] = x2
    hf = _rms(x2, gain_ref[...])
    hf_ref[...] = hf
    logits = lax.dot_general(rwt_ref[...], hf, (((1,), (1,)), ((), ())), precision=HIGHEST,
                             preferred_element_type=F32) + rb_ref[...]
    n_exp, tm = logits.shape
    expert = lax.broadcasted_iota(I32, logits.shape, 0)
    vals, picks = [], []
    for _ in range(TOP_K):
        best = jnp.max(logits, axis=0, keepdims=True)
        pick = jnp.min(jnp.where(logits == best, expert, n_exp), axis=0, keepdims=True)
        vals.append(best)
        picks.append(pick)
        logits = jnp.where(expert == pick, -jnp.inf, logits)
    idx_ref[...] = jnp.concatenate(picks, axis=0)
    es = [jnp.exp(v - vals[0]) for v in vals]
    total = functools.reduce(lambda a, b: a + b, es)
    gates = jnp.concatenate([e / total for e in es] + [jnp.zeros((LANES - TOP_K, tm), F32)], axis=0)
    gate_ref[...] = jnp.transpose(gates)


def _router(o_p, o_s, x1, wo, gain, rwt, rb):
    n, d = x1.shape
    tm = TOKEN_TILE
    npt = o_p.shape[0] // tm
    row = lambda i: (i, 0)
    return pl.pallas_call(
        functools.partial(_router_body, n_prompt_tiles=npt),
        grid=(n // tm,),
        in_specs=_dual(tm, d, npt) + [pl.BlockSpec((tm, d), row)]
        + [_const_spec(a.shape) for a in (wo, gain, rwt, rb)],
        out_specs=[pl.BlockSpec((tm, d), row), pl.BlockSpec((tm, d), row),
                   pl.BlockSpec((TOP_K, tm), lambda i: (0, i)), pl.BlockSpec((tm, LANES), row)],
        out_shape=[jax.ShapeDtypeStruct((n, d), F32), jax.ShapeDtypeStruct((n, d), F32),
                   jax.ShapeDtypeStruct((TOP_K, n), I32), jax.ShapeDtypeStruct((n, LANES), F32)],
        compiler_params=_params(1),
        name="router",
    )(o_p, o_s, x1, wo, gain, rwt, rb)


def _positions_body(idx_ref, pos_ref, counts_ref, rank_ref, *, n_experts):
    ch = CUMSUM_CHUNK
    n = idx_ref.shape[1]
    n_chunks = n // ch
    upper = (lax.broadcasted_iota(I32, (ch, ch), 0) <= lax.broadcasted_iota(I32, (ch, ch), 1)).astype(BF16)
    expert = lax.broadcasted_iota(I32, (n_experts, ch), 0)

    def onehot(k, c):
        cols = pl.ds(pl.multiple_of(c * ch, ch), ch)
        return cols, expert == idx_ref[k:k + 1, cols]

    running = jnp.zeros((n_experts, 1), F32)
    for k in range(TOP_K):
        def count(c, run, k=k):
            cols, oh = onehot(k, c)
            cum = jnp.dot(oh.astype(BF16), upper, preferred_element_type=F32)
            rank_ref[k:k + 1, cols] = jnp.sum(jnp.where(oh, cum - 1.0 + run, 0.0), axis=0, keepdims=True)
            return run + cum[:, ch - 1:ch]
        running = lax.fori_loop(0, n_chunks, count, running)

    counts = jnp.broadcast_to(running, (n_experts, LANES))
    counts_ref[...] = counts.astype(I32)
    lower = (lax.broadcasted_iota(I32, (n_experts, n_experts), 0)
             > lax.broadcasted_iota(I32, (n_experts, n_experts), 1)).astype(F32)
    start = jnp.dot(lower, counts, precision=HIGHEST, preferred_element_type=F32)[:, 0:1]

    for k in range(TOP_K):
        def place(c, carry, k=k):
            cols, oh = onehot(k, c)
            first = jnp.sum(jnp.where(oh, start, 0.0), axis=0, keepdims=True)
            pos_ref[k:k + 1, cols] = (rank_ref[k:k + 1, cols] + first).astype(I32)
            return carry
        lax.fori_loop(0, n_chunks, place, 0)


def _positions(idx, n_experts):
    k, n = idx.shape
    return pl.pallas_call(
        functools.partial(_positions_body, n_experts=n_experts),
        out_shape=[jax.ShapeDtypeStruct((k, n), I32), jax.ShapeDtypeStruct((n_experts, LANES), I32)],
        scratch_shapes=[pltpu.VMEM((k, n), F32)],
        compiler_params=pltpu.CompilerParams(vmem_limit_bytes=V7X_VMEM_REQUEST),
        name="positions",
    )(idx)


def _row_copy(src_ref, src_row, dst_ref, dst_row, sem):
    return pltpu.make_async_copy(src_ref.at[pl.ds(src_row, 1), :], dst_ref.at[pl.ds(dst_row, 1), :], sem)


def _dispatch_body(pos_ref, hf_ref, xs_ref, sem):
    tm = hf_ref.shape[0]

    def issue(t, carry):
        for k in range(TOP_K):
            _row_copy(hf_ref, t, xs_ref, pos_ref[k, t], sem).start(priority=k % 2)
        return carry

    lax.fori_loop(0, tm, issue, 0)

    def drain(t, carry):
        for k in range(TOP_K):
            _row_copy(hf_ref, t, xs_ref, pos_ref[k, t], sem).wait()
        return carry

    lax.fori_loop(0, tm, drain, 0)


def _dispatch(pos, hf):
    n, d = hf.shape
    tm = TOKEN_TILE
    return pl.pallas_call(
        _dispatch_body,
        grid=(n // tm,),
        in_specs=[pl.BlockSpec((TOP_K, tm), lambda i: (0, i), memory_space=pltpu.SMEM),
                  pl.BlockSpec((tm, d), lambda i: (i, 0))],
        out_specs=pl.BlockSpec(memory_space=pl.ANY),
        out_shape=jax.ShapeDtypeStruct((TOP_K * n, d), F32),
        scratch_shapes=[pltpu.SemaphoreType.DMA(())],
        compiler_params=_params(1),
        name="dispatch",
    )(pos, hf)


def _gmm_body(tile_ref, exp_ref, lo_ref, hi_ref, first_ref, slot_ref, next_ref, nitems_ref,
              xs_ref, w1_hbm, b1_ref, w2_hbm, b2_ref, ys_ref, w1f_ref, w2f_ref, w1b_ref, w2b_ref, sem):
    w = pl.program_id(0)

    def weight_copies(expert, slot):
        return (pltpu.make_async_copy(w1_hbm.at[0, expert], w1f_ref.at[slot], sem.at[0, slot]),
                pltpu.make_async_copy(w2_hbm.at[0, expert], w2f_ref.at[slot], sem.at[1, slot]))

    @pl.when(w < nitems_ref[0])
    def _():
        slot = slot_ref[w]
        new_expert = jnp.logical_or(w == 0, exp_ref[w] != exp_ref[jnp.maximum(w - 1, 0)])

        @pl.when(w == 0)
        def _():
            for cp in weight_copies(exp_ref[0], slot):
                cp.start()

        @pl.when(new_expert)
        def _():
            @pl.when(next_ref[w] >= 0)
            def _():
                for cp in weight_copies(next_ref[w], 1 - slot):
                    cp.start()

            for cp in weight_copies(exp_ref[w], slot):
                cp.wait()
            w1b_ref[...] = w1f_ref[slot].astype(BF16)
            w2b_ref[...] = w2f_ref[slot].astype(BF16)

        d_ff = w2b_ref.shape[0]
        hid = jnp.dot(xs_ref[...].astype(BF16), w1b_ref[...], preferred_element_type=F32) + b1_ref[0]
        glu = jnp.minimum(hid[:, :d_ff], SWIGLU_LIMIT)
        lin = jnp.clip(hid[:, d_ff:], -SWIGLU_LIMIT, SWIGLU_LIMIT)
        act = glu * jax.nn.sigmoid(SWIGLU_ALPHA * glu) * (lin + 1.0)
        y = jnp.dot(act.astype(BF16), w2b_ref[...], preferred_element_type=F32) + b2_ref[0]
        rows = lax.broadcasted_iota(I32, (ys_ref.shape[0], 1), 0)
        y = jnp.where((rows >= lo_ref[w]) & (rows < hi_ref[w]), y, 0.0)

        @pl.when(first_ref[w] == 1)
        def _():
            ys_ref[...] = y

        @pl.when(first_ref[w] == 0)
        def _():
            ys_ref[...] += y


def _group_work_items(counts, n_items):
    tr = MOE_ROW_TILE
    n_exp = counts.shape[0]
    experts = jnp.arange(n_exp, dtype=I32)
    ends = jnp.cumsum(counts)
    starts = ends - counts
    first_tile = starts // tr
    last_tile = jnp.maximum(ends - 1, 0) // tr
    per_expert = jnp.where(counts > 0, last_tile - first_tile + 1, 0)
    item_end = jnp.cumsum(per_expert)
    total = item_end[-1]
    w = jnp.minimum(jnp.arange(n_items, dtype=I32), total - 1)
    e = jnp.sum((item_end[None, :] <= w[:, None]).astype(I32), axis=1)
    onehot = (e[:, None] == experts[None, :]).astype(I32)
    pick = lambda a: jnp.sum(onehot * a[None, :], axis=1)
    tile = pick(first_tile) + (w - pick(item_end - per_expert))
    valid = jnp.arange(n_items, dtype=I32) < total
    lo = jnp.where(valid, jnp.maximum(pick(starts) - tile * tr, 0), 0)
    hi = jnp.where(valid, jnp.minimum(pick(ends) - tile * tr, tr), 0)
    first = jnp.concatenate([jnp.ones((1,), I32), (tile[1:] != tile[:-1]).astype(I32)])
    later = (experts[None, :] > experts[:, None]) & (per_expert[None, :] > 0)
    next_expert = jnp.min(jnp.where(later, experts[None, :], n_exp), axis=1)
    next_expert = jnp.where(next_expert < n_exp, next_expert, -1)
    rank = jnp.cumsum((per_expert > 0).astype(I32)) - 1
    as_i32 = lambda a: a.astype(I32)
    return (as_i32(tile), as_i32(e), as_i32(lo), as_i32(hi), as_i32(first), as_i32(pick(rank) % 2),
            as_i32(pick(next_expert)), as_i32(total).reshape(1))


def _gmm(xs, counts, w1, b1, w2, b2):
    n_rows, d = xs.shape
    _, n_exp, _, two_ff = w1.shape
    d_ff = two_ff // 2
    tr = MOE_ROW_TILE
    n_items = n_rows // tr + n_exp - 1
    items = _group_work_items(counts, n_items)
    grid_spec = pltpu.PrefetchScalarGridSpec(
        num_scalar_prefetch=len(items),
        grid=(n_items,),
        in_specs=[
            pl.BlockSpec((tr, d), lambda w, tile, *_: (tile[w], 0)),
            pl.BlockSpec(memory_space=pl.ANY),
            pl.BlockSpec((1, 1, two_ff), lambda w, tile, exp, *_: (exp[w], 0, 0)),
            pl.BlockSpec(memory_space=pl.ANY),
            pl.BlockSpec((1, 1, d), lambda w, tile, exp, *_: (exp[w], 0, 0)),
        ],
        out_specs=pl.BlockSpec((tr, d), lambda w, tile, *_: (tile[w], 0)),
        scratch_shapes=[pltpu.VMEM((2, d, two_ff), F32), pltpu.VMEM((2, d_ff, d), F32),
                        pltpu.VMEM((d, two_ff), BF16), pltpu.VMEM((d_ff, d), BF16),
                        pltpu.SemaphoreType.DMA((2, 2))],
    )
    return pl.pallas_call(
        _gmm_body,
        grid_spec=grid_spec,
        out_shape=jax.ShapeDtypeStruct((n_rows, d), F32),
        compiler_params=_params(1),
        name="moe_gmm",
    )(*items, xs, w1, b1.reshape(n_exp, 1, two_ff), w2, b2.reshape(n_exp, 1, d))


def _combine_body(pos_ref, x2_ref, gate_ref, gain_ref, ys_ref, yp_ref, ysm_ref, buf_ref, sem, *, n_prompt_tiles):
    tm = x2_ref.shape[0]

    def issue(t, carry):
        for k in range(TOP_K):
            _row_copy(ys_ref, pos_ref[k, t], buf_ref.at[k], t, sem).start(priority=k % 2)
        return carry

    lax.fori_loop(0, tm, issue, 0)

    def drain(t, carry):
        for k in range(TOP_K):
            _row_copy(ys_ref, pos_ref[k, t], buf_ref.at[k], t, sem).wait()
        return carry

    lax.fori_loop(0, tm, drain, 0)
    gate = gate_ref[...]
    x3 = x2_ref[...]
    for k in range(TOP_K):
        x3 = x3 + gate[:, k:k + 1] * buf_ref[k]
    y = _rms(x3, gain_ref[...])
    is_prompt = pl.program_id(0) < n_prompt_tiles

    @pl.when(is_prompt)
    def _():
        yp_ref[...] = y

    @pl.when(jnp.logical_not(is_prompt))
    def _():
        ysm_ref[...] = y


def _combine(pos, x2, gate_t, gain, ys, n_prompt_rows):
    n, d = x2.shape
    tm = TOKEN_TILE
    npt = n_prompt_rows // tm
    row = lambda i: (i, 0)
    return pl.pallas_call(
        functools.partial(_combine_body, n_prompt_tiles=npt),
        grid=(n // tm,),
        in_specs=[pl.BlockSpec((TOP_K, tm), lambda i: (0, i), memory_space=pltpu.SMEM),
                  pl.BlockSpec((tm, d), row), pl.BlockSpec((tm, LANES), row), _const_spec(gain.shape),
                  pl.BlockSpec(memory_space=pl.ANY)],
        out_specs=[pl.BlockSpec((tm, d), lambda i: (jnp.minimum(i, npt - 1), 0)),
                   pl.BlockSpec((tm, d), lambda i: (jnp.maximum(i - npt, 0), 0))],
        out_shape=[jax.ShapeDtypeStruct((n_prompt_rows, d), F32),
                   jax.ShapeDtypeStruct((n - n_prompt_rows, d), F32)],
        scratch_shapes=[pltpu.VMEM((TOP_K, tm, d), F32), pltpu.SemaphoreType.DMA(())],
        compiler_params=_params(1),
        name="moe_combine",
    )(pos, x2, gate_t, gain, ys)


def _head_group_lanes(per_head, hps):
    groups = per_head.reshape(-1, 1, hps)
    padded = jnp.pad(groups, ((0, 0), (0, 0), (0, LANES - hps)))
    return jnp.broadcast_to(padded, (groups.shape[0], SUBLANES, LANES)).reshape(-1, LANES)


def kernel(x_prompt, x_sample, mem_prompt, state_gdn, state_gdn_conv, state_conf_conv, cache_mem_k, cache_mem_v,
           norm_mix, w_in, gdn_conv_w, gdn_a_log, gdn_dt_bias, gdn_norm, gdn_o, conf_conv_w, conf_conv_b,
           conf_ln_g, conf_ln_b, conf_pw2, conf_pw2_b, w_out, norm_xa, norm_mem, xa_q, xa_kv, xa_o,
           norm_ffn, router_w, router_b, moe_w1, moe_b1, moe_w2, moe_b2, norm_final):
    n_batch, seq, d = x_prompt.shape
    n_seq, n_tok, _ = x_sample.shape
    depth, _, n_heads, dk, dv = state_gdn.shape
    assert depth == 1 and dk == LANES and dv == LANES
    mem_len, xa_heads = cache_mem_k.shape[2], cache_mem_k.shape[3]
    n_experts = router_w.shape[2]
    qkv_w = gdn_conv_w.shape[2]
    v_w = n_heads * dv
    hps = GDN_HEADS_PER_STEP
    n_hg = n_heads // hps
    n_gc = gdn_conv_w.shape[1] - 1
    n_cc = conf_conv_w.shape[1] - 1
    n_p, n_s = n_batch * seq, n_seq * n_tok
    assert seq % TOKEN_TILE == 0 and n_s % TOKEN_TILE == 0 and n_seq % SEQ_BLOCK == 0
    assert n_tok * SEQ_BLOCK <= LANES and n_heads % hps == 0 and 2 * hps <= LANES
    assert n_gc <= min(n_tok, SUBLANES) and n_cc <= CONF_HALO
    row = lambda a: a.reshape(1, -1)
    to_token_major = lambda a: a.reshape(n_seq, n_tok, -1).transpose(1, 0, 2).reshape(n_s, -1)
    to_seq_major = lambda a: a.reshape(n_tok, n_seq, -1).transpose(1, 0, 2).reshape(n_s, -1)

    xp = x_prompt.reshape(n_p, d)
    xs = to_token_major(x_sample)

    w = w_in.reshape(d, -1).astype(BF16)
    ab_lo, ab_hi = qkv_w + v_w, qkv_w + v_w + 2 * n_heads
    w_a = w[:, ab_lo:ab_lo + n_heads].reshape(d, n_hg, hps)
    w_b = w[:, ab_lo + n_heads:ab_hi].reshape(d, n_hg, hps)
    wab = jnp.pad(jnp.concatenate([w_a, w_b], axis=2), ((0, 0), (0, 0), (0, LANES - 2 * hps))).reshape(d, -1)
    qkv, z, ab, cv, sg = _inproj(xp, xs, row(norm_mix), w[:, :ab_lo], wab, w[:, ab_hi:ab_hi + 2 * d],
                                 w[:, ab_hi + 2 * d:], qkv_w)

    alog = _head_group_lanes(gdn_a_log.reshape(-1), hps)
    dtb = _head_group_lanes(gdn_dt_bias.reshape(-1), hps)
    gnorm = row(gdn_norm)
    og_p, s_p, tail_q, tail_k, tail_v = _gdn_prompt(qkv, z, ab, gdn_conv_w, alog, dtb, gnorm,
                                                    n_batch, seq, n_heads, dk)
    og_s, s_s = _gdn_sample(qkv, z, ab, state_gdn_conv, state_gdn, gdn_conv_w, alog, dtb, gnorm,
                            n_p, n_seq, n_tok, n_heads, dk)

    conf_args = (conf_conv_w, row(conf_conv_b), row(conf_ln_g), row(conf_ln_b))
    yb_p, cconv_tail = _conf_prompt(cv, *conf_args, n_batch, seq)
    yb_s, cconv_s = _conf_sample(cv, state_conf_conv, *conf_args, n_p, n_seq, n_tok)

    bf = lambda a: a.reshape(a.shape[1:]).astype(BF16)
    x1, q = _merge(og_p, og_s, yb_p, yb_s, sg, xp, xs, bf(gdn_o), bf(conf_pw2), row(conf_pw2_b), bf(w_out),
                   row(norm_xa), bf(xa_q))

    mk_p, mv_p = _memkv(mem_prompt.reshape(n_batch * mem_len, d), row(norm_mem), bf(xa_kv))
    o_p = _attn_prompt(q, mk_p, mv_p, n_batch, seq, mem_len, xa_heads)
    o_s = to_token_major(_attn_sample(to_seq_major(q[n_p:]), cache_mem_k, cache_mem_v, n_tok))

    x2, hf, idx, gate_t = _router(o_p, o_s, x1, bf(xa_o), row(norm_ffn), router_w.reshape(d, n_experts).T,
                                  router_b.reshape(-1, 1))
    pos, counts = _positions(idx, n_experts)
    xsorted = _dispatch(pos, hf)
    ysorted = _gmm(xsorted, counts[:, 0], moe_w1, moe_b1, moe_w2, moe_b2)
    y_p, y_s = _combine(pos, x2, gate_t, row(norm_final), ysorted, n_p)

    y_prompt = y_p.reshape(n_batch, seq, d)
    y_sample = to_seq_major(y_s).reshape(n_seq, n_tok, d)
    gconv_p = jnp.concatenate([tail_q, tail_k, tail_v], axis=2)[:, SUBLANES - n_gc:, :]
    gconv_s = to_seq_major(qkv[n_p:]).reshape(n_seq, n_tok, qkv_w)[:, n_tok - n_gc:, :].astype(F32)
    cconv_p = cconv_tail[:, CONF_HALO - n_cc:, :]
    kv_shape = (1, n_batch, mem_len, xa_heads, d // xa_heads)
    return (y_prompt, y_sample, s_p[None], gconv_p[None], cconv_p[None], mk_p.reshape(kv_shape),
            mv_p.reshape(kv_shape), s_s[None], gconv_s[None], cconv_s[None])
```
